```python
import functools
import jax, jax.numpy as jnp
from jax import lax
import numpy as np

D_MODEL = 1024
BATCH = 8
SEQ = 2048
DEPTH = 1
DEC_BATCH = 32
DEC_SEQ = 4
PAST_LEN = 8192
PAGE_SIZE = 128

MIX_W = D_MODEL
POOL_W = MIX_W // 2
POOL_WINDOWS = (2, 4, 8, 16)
N_POOL_GROUPS = 4
POOL_GW = POOL_W // N_POOL_GROUPS
POOL_STATE = 15
ATT_W = MIX_W - POOL_W
HEAD_DIM = 128
N_HEADS = ATT_W // HEAD_DIM
IDX_HEADS = 8
IDX_DIM = 64
TOPK_MAX = 256
ROPE_THETA = 10000.0
Q_BLOCK = 128
N_KEYS = 128
N_EXPERTS = N_KEYS * N_KEYS
PEER_HEADS = 8
PEER_TOPK = 16
PEER_QDIM = 256
PEER_HALF = PEER_QDIM // 2
PEER_BLOCK = 128
EPS = 1e-6
SPLIT_SIZES = (POOL_W, N_HEADS * HEAD_DIM, HEAD_DIM, HEAD_DIM, IDX_HEADS * IDX_DIM, IDX_DIM, IDX_HEADS)
IN_W = POOL_W + N_HEADS * HEAD_DIM + 2 * HEAD_DIM + IDX_HEADS * IDX_DIM + IDX_DIM + IDX_HEADS

kernel_name = "hybrid_pool_dsa_peer_step"


def rms_norm(x, g):
    xf = x.astype(jnp.float32)
    y = xf * lax.rsqrt(jnp.mean(xf * xf, axis=-1, keepdims=True) + EPS)
    return (y * g.astype(jnp.float32)).astype(x.dtype)


def rope(x, pos):
    half = x.shape[-1] // 2
    inv = ROPE_THETA ** (-jnp.arange(half, dtype=jnp.float32) / half)
    ang = pos.astype(jnp.float32)[:, None] * inv[None, :]
    cos = jnp.cos(ang)[None, :, None, :]
    sin = jnp.sin(ang)[None, :, None, :]
    xf = x.astype(jnp.float32)
    x1, x2 = xf[..., :half], xf[..., half:]
    return jnp.concatenate([x1 * cos - x2 * sin, x2 * cos + x1 * sin], axis=-1).astype(x.dtype)


def project(h, pos, w_in, q_norm_g, k_norm_g):
    B, T, _ = h.shape
    z = h @ w_in
    offs = np.cumsum(np.array(SPLIT_SIZES))[:-1].tolist()
    u, q, k, v, iq, ik, iw = jnp.split(z, offs, axis=-1)
    q = rope(rms_norm(q.reshape(B, T, N_HEADS, HEAD_DIM), q_norm_g), pos)
    k = rope(rms_norm(k[:, :, None, :], k_norm_g), pos)[:, :, 0]
    iq = rope(iq.reshape(B, T, IDX_HEADS, IDX_DIM), pos)
    ik = rope(ik[:, :, None, :], pos)[:, :, 0]
    iw = iw * (IDX_HEADS ** -0.5)
    return u, q, k, v, iq, ik, iw


def pool_mix(u, prefix, start_pos, w_pool, pool_scale):
    T = u.shape[1]
    full = jnp.concatenate([prefix, u], axis=1)
    c = jnp.cumsum(full.astype(jnp.float32), axis=1)
    c = jnp.pad(c, ((0, 0), (1, 0), (0, 0)))
    pos = start_pos + jnp.arange(T)
    outs = []
    for g, w in enumerate(POOL_WINDOWS):
        sl = slice(g * POOL_GW, (g + 1) * POOL_GW)
        hi = c[:, POOL_STATE + 1:POOL_STATE + 1 + T, sl]
        lo = c[:, POOL_STATE + 1 - w:POOL_STATE + 1 - w + T, sl]
        cnt = jnp.minimum(pos + 1, w).astype(jnp.float32)[None, :, None]
        d = ((hi - lo) / cnt - u[..., sl].astype(jnp.float32)).astype(u.dtype)
        outs.append(d @ w_pool[g])
    y = jnp.concatenate(outs, axis=-1) * pool_scale
    return y, full[:, -POOL_STATE:]


def take_rows(a, idx):
    return jax.vmap(lambda ab, ib: ab[ib])(a, idx)


def dsa_block(q, iq, iw, qpos, idx_keys, k_sel, gather_kv):
    L = idx_keys.shape[1]
    s = jnp.einsum('bqhd,bld->bqhl', iq, idx_keys)
    score = jnp.einsum('bqhl,bqh->bql', jax.nn.relu(s), iw).astype(jnp.float32)
    adm = jnp.arange(L)[None, None, :] <= qpos[None, :, None]
    score = jnp.where(adm, score, -jnp.inf)
    top_val, top_idx = lax.top_k(score, k_sel)
    valid = jnp.isfinite(top_val)
    k_rows, v_rows = gather_kv(top_idx)
    logits = jnp.einsum('bqhd,bqkd->bqhk', q, k_rows).astype(jnp.float32) * (HEAD_DIM ** -0.5)
    logits = jnp.where(valid[:, :, None, :], logits, -jnp.inf)
    p = jax.nn.softmax(logits, axis=-1).astype(v_rows.dtype)
    return jnp.einsum('bqhk,bqkd->bqhd', p, v_rows)


def dsa_prompt(q, k, v, iq, ik, iw):
    B, S = q.shape[:2]
    k_sel = min(TOPK_MAX, S // 4)
    nb = S // Q_BLOCK

    def to_blocks(a):
        return a.reshape(B, nb, Q_BLOCK, *a.shape[2:]).swapaxes(0, 1)

    qpos = jnp.arange(S).reshape(nb, Q_BLOCK)
    gather = lambda idx: (take_rows(k, idx), take_rows(v, idx))

    def body(args):
        qb, iqb, iwb, pb = args
        return dsa_block(qb, iqb, iwb, pb, ik, k_sel, gather)

    out = lax.map(body, (to_blocks(q), to_blocks(iq), to_blocks(iw), qpos))
    return out.swapaxes(0, 1).reshape(B, S, N_HEADS * HEAD_DIM)


def dsa_sample(q, k, v, iq, ik, iw, cache_k, cache_v, cache_ik, page_table):
    DB, DS = q.shape[:2]
    P = page_table.shape[1] * PAGE_SIZE
    past_ik = cache_ik[page_table].reshape(DB, P, IDX_DIM)
    idx_keys = jnp.concatenate([past_ik, ik], axis=1)
    k_sel = min(TOPK_MAX, (P + DS) // 4)

    def gather(idx):
        in_past = (idx < P)[..., None]
        pidx = jnp.minimum(idx, P - 1)
        phys = jax.vmap(lambda pt, i: pt[i])(page_table, pidx // PAGE_SIZE)
        off = pidx % PAGE_SIZE
        nidx = jnp.clip(idx - P, 0, DS - 1)
        k_rows = jnp.where(in_past, cache_k[phys, off], take_rows(k, nidx))
        v_rows = jnp.where(in_past, cache_v[phys, off], take_rows(v, nidx))
        return k_rows, v_rows

    qpos = P + jnp.arange(DS)
    out = dsa_block(q, iq, iw, qpos, idx_keys, k_sel, gather)
    return out.reshape(DB, DS, N_HEADS * HEAD_DIM)


def peer(h, w_pq, sub_keys, expert_u, expert_v):
    B, T, D = h.shape
    hf = h.reshape(-1, D)
    N = hf.shape[0]
    q = (hf @ w_pq).reshape(N, PEER_HEADS, 2, PEER_HALF)
    s = jnp.einsum('nhcd,hckd->nhck', q, sub_keys).astype(jnp.float32)
    sv, si = lax.top_k(s, PEER_TOPK)
    cand = sv[:, :, 0, :, None] + sv[:, :, 1, None, :]
    cidx = si[:, :, 0, :, None] * N_KEYS + si[:, :, 1, None, :]
    fv, fi = lax.top_k(cand.reshape(N, PEER_HEADS, PEER_TOPK * PEER_TOPK), PEER_TOPK)
    eidx = jnp.take_along_axis(cidx.reshape(N, PEER_HEADS, PEER_TOPK * PEER_TOPK), fi, axis=-1)
    g = jax.nn.softmax(fv, axis=-1).astype(h.dtype)
    E = PEER_HEADS * PEER_TOPK
    eidx = eidx.reshape(N, E)
    g = g.reshape(N, E)
    nb = -(-N // PEER_BLOCK)
    pad = nb * PEER_BLOCK - N
    xb = jnp.pad(hf, ((0, pad), (0, 0))).reshape(nb, PEER_BLOCK, D)
    eb = jnp.pad(eidx, ((0, pad), (0, 0))).reshape(nb, PEER_BLOCK, E)
    gb = jnp.pad(g, ((0, pad), (0, 0))).reshape(nb, PEER_BLOCK, E)

    def body(args):
        xt, et, gt = args
        a = jax.nn.gelu(jnp.einsum('td,ted->te', xt, expert_u[et]), approximate=False)
        return jnp.einsum('te,ted->td', gt * a, expert_v[et])

    out = lax.map(body, (xb, eb, gb)).reshape(-1, D)[:N]
    return out.reshape(B, T, D)


def trunk_layer(x, pos0, pool_prefix, attend, norm1_g, w_in, q_norm_g, k_norm_g, w_pool,
                pool_scale, w_o, norm2_g, w_pq, sub_keys, expert_u, expert_v):
    T = x.shape[1]
    pos = pos0 + jnp.arange(T)
    h = rms_norm(x, norm1_g)
    u, q, k, v, iq, ik, iw = project(h, pos, w_in, q_norm_g, k_norm_g)
    ya, pool_state = pool_mix(u, pool_prefix, pos0, w_pool, pool_scale)
    yb = attend(q, k, v, iq, ik, iw)
    x = x + jnp.concatenate([ya, yb], axis=-1) @ w_o
    x = x + peer(rms_norm(x, norm2_g), w_pq, sub_keys, expert_u, expert_v)
    return x, k, v, ik, pool_state


def setup_inputs(seed: int = 0) -> dict:
    key = jax.random.key(seed)
    ks = jax.random.split(key, 24)
    f32 = jnp.float32
    n_pages = PAST_LEN // PAGE_SIZE
    n_used = DEC_BATCH * n_pages
    n_pool = (n_used * 5 + 3) // 4
    nrm = lambda k, shape, s: jax.random.normal(k, shape, f32) * s
    perm = jax.random.permutation(ks[0], n_pool)[:n_used]
    return {
        "x_prompt": nrm(ks[1], (BATCH, SEQ, D_MODEL), 1.0),
        "x_sample": nrm(ks[2], (DEC_BATCH, DEC_SEQ, D_MODEL), 1.0),
        "cache_k": nrm(ks[3], (DEPTH, n_pool, PAGE_SIZE, HEAD_DIM), 1.0),
        "cache_v": nrm(ks[4], (DEPTH, n_pool, PAGE_SIZE, HEAD_DIM), 1.0),
        "cache_ik": nrm(ks[5], (DEPTH, n_pool, PAGE_SIZE, IDX_DIM), 1.0),
        "state_pool": nrm(ks[6], (DEPTH, DEC_BATCH, POOL_STATE, POOL_W), 1.0),
        "page_table": perm.reshape(DEC_BATCH, n_pages).astype(jnp.int32),
        "norm1_g": 1.0 + nrm(ks[7], (DEPTH, D_MODEL), 0.02),
        "w_in": nrm(ks[8], (DEPTH, D_MODEL, IN_W), D_MODEL ** -0.5),
        "q_norm_g": 1.0 + nrm(ks[9], (DEPTH, HEAD_DIM), 0.02),
        "k_norm_g": 1.0 + nrm(ks[10], (DEPTH, HEAD_DIM), 0.02),
        "w_pool": nrm(ks[11], (DEPTH, N_POOL_GROUPS, POOL_GW, POOL_GW), POOL_GW ** -0.5),
        "pool_scale": 1.0 + nrm(ks[12], (DEPTH, POOL_W), 0.1),
        "w_o": nrm(ks[13], (DEPTH, MIX_W, D_MODEL), MIX_W ** -0.5),
        "norm2_g": 1.0 + nrm(ks[14], (DEPTH, D_MODEL), 0.02),
        "w_pq": nrm(ks[15], (DEPTH, D_MODEL, PEER_HEADS * PEER_QDIM), D_MODEL ** -0.5),
        "sub_keys": nrm(ks[16], (DEPTH, PEER_HEADS, 2, N_KEYS, PEER_HALF), PEER_HALF ** -0.5),
        "expert_u": nrm(ks[17], (DEPTH, N_EXPERTS, D_MODEL), D_MODEL ** -0.5),
        "expert_v": nrm(ks[18], (DEPTH, N_EXPERTS, D_MODEL), 0.35),
    }


def reference(x_prompt, x_sample, cache_k, cache_v, cache_ik, state_pool, page_table,
              norm1_g, w_in, q_norm_g, k_norm_g, w_pool, pool_scale, w_o, norm2_g,
              w_pq, sub_keys, expert_u, expert_v):
    past = page_table.shape[1] * PAGE_SIZE
    yp, ys = x_prompt, x_sample
    kp, vp, ikp, pp, kd, vd, ikd, pd = [], [], [], [], [], [], [], []
    for l in range(DEPTH):
        lw = (norm1_g[l], w_in[l], q_norm_g[l], k_norm_g[l], w_pool[l], pool_scale[l],
              w_o[l], norm2_g[l], w_pq[l], sub_keys[l], expert_u[l], expert_v[l])
        prefix = jnp.zeros((yp.shape[0], POOL_STATE, POOL_W), yp.dtype)
        yp, k1, v1, ik1, p1 = trunk_layer(yp, 0, prefix, dsa_prompt, *lw)
        attend_s = functools.partial(dsa_sample, cache_k=cache_k[l], cache_v=cache_v[l],
                                     cache_ik=cache_ik[l], page_table=page_table)
        ys, k2, v2, ik2, p2 = trunk_layer(ys, past, state_pool[l], attend_s, *lw)
        kp.append(k1); vp.append(v1); ikp.append(ik1); pp.append(p1)
        kd.append(k2); vd.append(v2); ikd.append(ik2); pd.append(p2)
    return (yp, ys, jnp.stack(kp), jnp.stack(vp), jnp.stack(ikp), jnp.stack(pp),
            jnp.stack(kd), jnp.stack(vd), jnp.stack(ikd), jnp.stack(pd))
```

```python
import functools
import math

import jax
import jax.numpy as jnp
import numpy as np
from jax import lax
from jax.experimental import pallas as pl
from jax.experimental.pallas import tpu as pltpu

F32 = jnp.float32
BF16 = jnp.bfloat16
I32 = jnp.int32

LANES = 128
SUBLANES = 8
VMEM_LIMIT_BYTES = 56 * 1024 * 1024

D_MODEL = 1024
POOL_W = 512
POOL_WINDOWS = (2, 4, 8, 16)
POOL_GW = 128
POOL_STATE = 15
HEAD_DIM = 128
N_HEADS = 4
IDX_HEADS = 8
IDX_DIM = 64
TOPK_MAX = 256
ROPE_THETA = 10000.0
PAGE_SIZE = 128
N_KEYS = 128
PEER_HEADS = 8
PEER_TOPK = 16
PEER_HALF = 128
EPS = 1e-6
NEG_INF = float("-inf")


def _cparams(sem):
    return pltpu.CompilerParams(dimension_semantics=sem, vmem_limit_bytes=VMEM_LIMIT_BYTES)


PEER_TT = 128
CAND_ROWS = 16 + 8 * 15


def _top16_rows(x, iota0):
    vals, idxs = [], []
    for _ in range(PEER_TOPK):
        m = jnp.max(x, axis=0, keepdims=True)
        idx = jnp.min(jnp.where(x == m, iota0, N_KEYS), axis=0, keepdims=True)
        x = jnp.where(iota0 == idx, NEG_INF, x)
        vals.append(m)
        idxs.append(idx)
    return vals, idxs


def _peer_select_kernel(h2t_ref, wpqt_ref, sk_ref, n1_ref, w1_ref, r2_ref, p2_ref, s_scr):
    tt = h2t_ref.shape[1]
    qt = jnp.dot(wpqt_ref[...], h2t_ref[...], preferred_element_type=F32).astype(BF16)
    for hc in range(2 * PEER_HEADS):
        s_scr[hc] = jnp.dot(sk_ref[hc], qt[hc * PEER_HALF:(hc + 1) * PEER_HALF, :],
                            preferred_element_type=F32)

    iota0 = lax.broadcasted_iota(I32, (N_KEYS, tt), 0)
    rows = lax.broadcasted_iota(I32, (CAND_ROWS, tt), 0)
    ca = jnp.where(rows < 16, 0, ((rows - 16) >> 3) + 1)
    cb = jnp.where(rows < 16, rows, (rows - 16) & 7)
    cvalid = (ca + 1) * (cb + 1) <= PEER_TOPK
    cflat = ca * PEER_TOPK + cb

    def head(h, carry):
        sv1, si1 = _top16_rows(s_scr[2 * h], iota0)
        x2 = s_scr[2 * h + 1]
        r2 = jnp.full((N_KEYS, tt), float(N_KEYS - 1), F32)
        p2 = jnp.zeros((N_KEYS, tt), F32)
        sv2 = []
        for r in range(PEER_TOPK):
            m = jnp.max(x2, axis=0, keepdims=True)
            idx = jnp.min(jnp.where(x2 == m, iota0, N_KEYS), axis=0, keepdims=True)
            hit = iota0 == idx
            x2 = jnp.where(hit, NEG_INF, x2)
            sv2.append(m)
            r2 = jnp.where(hit, float(r), r2)
            p2 = jnp.where(hit, jnp.exp(m - sv2[0]), p2)
        e1 = [jnp.exp(v - sv1[0]) for v in sv1]
        sv2a = jnp.concatenate(sv2, axis=0)
        e2a = jnp.exp(sv2a - sv2[0])
        cand = jnp.concatenate([sv1[0] + sv2a] + [sv1[a] + sv2a[:8] for a in range(1, 16)], axis=0)
        egrid = jnp.concatenate([e1[0] * e2a] + [e1[a] * e2a[:8] for a in range(1, 16)], axis=0)
        cand = jnp.where(cvalid, cand, NEG_INF)
        sel = jnp.zeros((CAND_ROWS, tt), F32)
        for _ in range(PEER_TOPK):
            m = jnp.max(cand, axis=0, keepdims=True)
            f = jnp.min(jnp.where(cand == m, cflat, PEER_TOPK * PEER_TOPK), axis=0, keepdims=True)
            hit = cflat == f
            cand = jnp.where(hit, NEG_INF, cand)
            sel = jnp.where(hit, 1.0, sel)
        z = jnp.sum(sel * egrid, axis=0, keepdims=True)
        inv_z = 1.0 / z
        n1 = jnp.zeros((N_KEYS, tt), F32)
        w1 = jnp.zeros((N_KEYS, tt), F32)
        for a in range(PEER_TOPK):
            lo = 0 if a == 0 else 16 + 8 * (a - 1)
            hi = 16 if a == 0 else lo + 8
            n_a = jnp.sum(sel[lo:hi], axis=0, keepdims=True)
            hit = iota0 == si1[a]
            n1 = jnp.where(hit, n_a, n1)
            w1 = jnp.where(hit, e1[a] * inv_z, w1)
        n1_ref[h] = n1
        w1_ref[h] = w1
        r2_ref[h] = r2.astype(BF16)
        p2_ref[h] = p2.astype(BF16)
        return carry

    lax.fori_loop(0, PEER_HEADS, head, 0)


def _peer_select(h2t, wpqt, sk):
    t = h2t.shape[1]
    tt = PEER_TT
    assert t % tt == 0
    tok = lambda i: (0, 0, i)
    shp = (PEER_HEADS, N_KEYS, t)
    return pl.pallas_call(
        _peer_select_kernel,
        grid=(t // tt,),
        in_specs=[
            pl.BlockSpec((D_MODEL, tt), lambda i: (0, i)),
            pl.BlockSpec(wpqt.shape, lambda i: (0, 0)),
            pl.BlockSpec(sk.shape, lambda i: (0, 0, 0)),
        ],
        out_specs=[pl.BlockSpec((PEER_HEADS, N_KEYS, tt), tok)] * 4,
        out_shape=[jax.ShapeDtypeStruct(shp, F32), jax.ShapeDtypeStruct(shp, F32),
                   jax.ShapeDtypeStruct(shp, BF16), jax.ShapeDtypeStruct(shp, BF16)],
        scratch_shapes=[pltpu.VMEM((2 * PEER_HEADS, N_KEYS, tt), F32)],
        compiler_params=_cparams(("arbitrary",)),
        name="peer_select",
    )(h2t, wpqt, sk)


PEER_TM = 512
PEER_TE = 1024


def _gelu_exact(x):
    return 0.5 * x * (1.0 + lax.erf(x * (1.0 / math.sqrt(2.0))))


def _peer_expert_kernel(xt_ref, u_ref, vt_ref, n1_ref, w1_ref, r2_ref, p2_ref, x2_ref, out_ref,
                        acc_ref, m_ref):
    j = pl.program_id(1)
    nc = PEER_TE // N_KEYS
    tm = xt_ref.shape[1]

    @pl.when(j == 0)
    def _():
        acc_ref[...] = jnp.zeros_like(acc_ref)

    s = jnp.dot(u_ref[...], xt_ref[...], preferred_element_type=F32)
    c0 = pl.multiple_of(j * nc, nc)
    n_rows = [n1_ref[h, pl.ds(c0, nc), :].astype(BF16) for h in range(PEER_HEADS)]
    w_rows = [w1_ref[h, pl.ds(c0, nc), :].astype(BF16) for h in range(PEER_HEADS)]
    for cc in range(nc):
        a = _gelu_exact(s[cc * N_KEYS:(cc + 1) * N_KEYS, :])
        g = jnp.zeros((N_KEYS, tm), BF16)
        for h in range(PEER_HEADS):
            nb = jnp.broadcast_to(n_rows[h][cc:cc + 1, :], (N_KEYS, tm))
            wb = jnp.broadcast_to(w_rows[h][cc:cc + 1, :], (N_KEYS, tm))
            g = g + jnp.where(r2_ref[h] < nb, p2_ref[h], jnp.zeros((), BF16)) * wb
        m_ref[cc * N_KEYS:(cc + 1) * N_KEYS, :] = a.astype(BF16) * g
    acc_ref[...] += jnp.dot(vt_ref[...], m_ref[...], preferred_element_type=F32)

    @pl.when(j == pl.num_programs(1) - 1)
    def _():
        out_ref[...] = x2_ref[...] + acc_ref[...].T


def _peer_experts(xt, u, vt, n1, w1, r2, p2, x2):
    d, t = xt.shape
    e = u.shape[0]
    tm, te = PEER_TM, PEER_TE
    assert t % tm == 0 and e % te == 0
    fac = pl.BlockSpec((PEER_HEADS, N_KEYS, tm), lambda i, j: (0, 0, i))
    return pl.pallas_call(
        _peer_expert_kernel,
        grid=(t // tm, e // te),
        in_specs=[
            pl.BlockSpec((d, tm), lambda i, j: (0, i)),
            pl.BlockSpec((te, d), lambda i, j: (j, 0)),
            pl.BlockSpec((d, te), lambda i, j: (0, j)),
            fac, fac, fac, fac,
            pl.BlockSpec((tm, d), lambda i, j: (i, 0)),
        ],
        out_specs=pl.BlockSpec((tm, d), lambda i, j: (i, 0)),
        out_shape=jax.ShapeDtypeStruct((t, d), F32),
        scratch_shapes=[pltpu.VMEM((d, tm), F32), pltpu.VMEM((te, tm), BF16)],
        compiler_params=_cparams(("arbitrary", "arbitrary")),
        name="peer_experts",
    )(xt, u, vt, n1, w1, r2, p2, x2)


def _peer(x2, h2t, w_pq, sub_keys, expert_u, expert_v):
    wpqt = w_pq.T.astype(BF16)
    sk = sub_keys.reshape(2 * PEER_HEADS, N_KEYS, PEER_HALF).astype(BF16)
    n1, w1, r2, p2 = _peer_select(h2t, wpqt, sk)
    return _peer_experts(h2t, expert_u.astype(BF16), expert_v.T.astype(BF16), n1, w1, r2, p2, x2)


SPLIT_SIZES = (POOL_W, N_HEADS * HEAD_DIM, HEAD_DIM, HEAD_DIM, IDX_HEADS * IDX_DIM, IDX_DIM, IDX_HEADS)


def _rms_norm(x, g):
    xf = x.astype(F32)
    y = xf * lax.rsqrt(jnp.mean(xf * xf, axis=-1, keepdims=True) + EPS)
    return (y * g.astype(F32)).astype(x.dtype)


def _rope(x, pos):
    half = x.shape[-1] // 2
    inv = ROPE_THETA ** (-jnp.arange(half, dtype=F32) / half)
    ang = pos.astype(F32)[:, None] * inv[None, :]
    cos = jnp.cos(ang)[None, :, None, :]
    sin = jnp.sin(ang)[None, :, None, :]
    x1, x2 = x[..., :half], x[..., half:]
    return jnp.concatenate([x1 * cos - x2 * sin, x2 * cos + x1 * sin], axis=-1)


def _project(h, pos, w_in, q_norm_g, k_norm_g):
    B, T, _ = h.shape
    z = h @ w_in
    offs = np.cumsum(np.array(SPLIT_SIZES))[:-1].tolist()
    u, q, k, v, iq, ik, iw = jnp.split(z, offs, axis=-1)
    q = _rope(_rms_norm(q.reshape(B, T, N_HEADS, HEAD_DIM), q_norm_g), pos)
    k = _rope(_rms_norm(k[:, :, None, :], k_norm_g), pos)[:, :, 0]
    iq = _rope(iq.reshape(B, T, IDX_HEADS, IDX_DIM), pos)
    ik = _rope(ik[:, :, None, :], pos)[:, :, 0]
    iw = iw * (IDX_HEADS ** -0.5)
    return u, q, k, v, iq, ik, iw


def _pool_mix(u, prefix, start_pos, w_pool, pool_scale):
    T = u.shape[1]
    full = jnp.concatenate([prefix, u], axis=1)
    c = jnp.cumsum(full.astype(F32), axis=1)
    c = jnp.pad(c, ((0, 0), (1, 0), (0, 0)))
    pos = start_pos + jnp.arange(T)
    outs = []
    for g, w in enumerate(POOL_WINDOWS):
        sl = slice(g * POOL_GW, (g + 1) * POOL_GW)
        hi = c[:, POOL_STATE + 1:POOL_STATE + 1 + T, sl]
        lo = c[:, POOL_STATE + 1 - w:POOL_STATE + 1 - w + T, sl]
        cnt = jnp.minimum(pos + 1, w).astype(F32)[None, :, None]
        d = ((hi - lo) / cnt - u[..., sl].astype(F32)).astype(u.dtype)
        outs.append(d @ w_pool[g])
    y = jnp.concatenate(outs, axis=-1) * pool_scale
    return y, full[:, -POOL_STATE:]


def _take_rows(a, idx):
    return jax.vmap(lambda ab, ib: ab[ib])(a, idx)


def _dsa_block(q, iq, iw, qpos, idx_keys, k_sel, gather_kv):
    L = idx_keys.shape[1]
    s = jnp.einsum('bqhd,bld->bqhl', iq, idx_keys)
    score = jnp.einsum('bqhl,bqh->bql', jax.nn.relu(s), iw).astype(F32)
    adm = jnp.arange(L)[None, None, :] <= qpos[None, :, None]
    score = jnp.where(adm, score, -jnp.inf)
    top_val, top_idx = lax.top_k(score, k_sel)
    valid = jnp.isfinite(top_val)
    k_rows, v_rows = gather_kv(top_idx)
    logits = jnp.einsum('bqhd,bqkd->bqhk', q, k_rows).astype(F32) * (HEAD_DIM ** -0.5)
    logits = jnp.where(valid[:, :, None, :], logits, -jnp.inf)
    p = jax.nn.softmax(logits, axis=-1).astype(v_rows.dtype)
    return jnp.einsum('bqhk,bqkd->bqhd', p, v_rows)


def _dsa_prompt(q, k, v, iq, ik, iw):
    B, S = q.shape[:2]
    k_sel = min(TOPK_MAX, S // 4)
    QB = 128
    nb = S // QB

    def to_blocks(a):
        return a.reshape(B, nb, QB, *a.shape[2:]).swapaxes(0, 1)

    qpos = jnp.arange(S).reshape(nb, QB)
    gather = lambda idx: (_take_rows(k, idx), _take_rows(v, idx))

    def body(args):
        qb, iqb, iwb, pb = args
        return _dsa_block(qb, iqb, iwb, pb, ik, k_sel, gather)

    out = lax.map(body, (to_blocks(q), to_blocks(iq), to_blocks(iw), qpos))
    return out.swapaxes(0, 1).reshape(B, S, N_HEADS * HEAD_DIM)


def _dsa_sample(q, k, v, iq, ik, iw, cache_k, cache_v, cache_ik, page_table):
    DB, DS = q.shape[:2]
    P = page_table.shape[1] * PAGE_SIZE
    past_ik = cache_ik[page_table].reshape(DB, P, IDX_DIM)
    idx_keys = jnp.concatenate([past_ik, ik], axis=1)
    k_sel = min(TOPK_MAX, (P + DS) // 4)

    def gather(idx):
        in_past = (idx < P)[..., None]
        pidx = jnp.minimum(idx, P - 1)
        phys = jax.vmap(lambda pt, i: pt[i])(page_table, pidx // PAGE_SIZE)
        off = pidx % PAGE_SIZE
        nidx = jnp.clip(idx - P, 0, DS - 1)
        k_rows = jnp.where(in_past, cache_k[phys, off], _take_rows(k, nidx))
        v_rows = jnp.where(in_past, cache_v[phys, off], _take_rows(v, nidx))
        return k_rows, v_rows

    qpos = P + jnp.arange(DS)
    out = _dsa_block(q, iq, iw, qpos, idx_keys, k_sel, gather)
    return out.reshape(DB, DS, N_HEADS * HEAD_DIM)


def _front(x, pos0, pool_prefix, attend, norm1_g, w_in, q_norm_g, k_norm_g, w_pool, pool_scale, w_o,
           norm2_g):
    T = x.shape[1]
    pos = pos0 + jnp.arange(T)
    h = _rms_norm(x, norm1_g)
    u, q, k, v, iq, ik, iw = _project(h, pos, w_in, q_norm_g, k_norm_g)
    ya, pool_state = _pool_mix(u, pool_prefix, pos0, w_pool, pool_scale)
    yb = attend(q, k, v, iq, ik, iw)
    x = x + jnp.concatenate([ya, yb], axis=-1) @ w_o
    h2 = _rms_norm(x, norm2_g)
    return x, h2, k, v, ik, pool_state


def kernel(x_prompt, x_sample, cache_k, cache_v, cache_ik, state_pool, page_table, norm1_g, w_in,
           q_norm_g, k_norm_g, w_pool, pool_scale, w_o, norm2_g, w_pq, sub_keys, expert_u, expert_v):
    past = page_table.shape[1] * PAGE_SIZE
    lw = (norm1_g[0], w_in[0], q_norm_g[0], k_norm_g[0], w_pool[0], pool_scale[0], w_o[0], norm2_g[0])
    prefix = jnp.zeros((x_prompt.shape[0], POOL_STATE, POOL_W), x_prompt.dtype)
    xp, hp, k1, v1, ik1, p1 = _front(x_prompt, 0, prefix, _dsa_prompt, *lw)
    attend_s = functools.partial(_dsa_sample, cache_k=cache_k[0], cache_v=cache_v[0],
                                 cache_ik=cache_ik[0], page_table=page_table)
    xs, hs, k2, v2, ik2, p2 = _front(x_sample, past, state_pool[0], attend_s, *lw)
    n_p = xp.shape[0] * xp.shape[1]
    n_s = xs.shape[0] * xs.shape[1]
    t_pad = -(-(n_p + n_s) // PEER_TM) * PEER_TM
    x2 = jnp.concatenate([xp.reshape(n_p, D_MODEL), xs.reshape(n_s, D_MODEL)], axis=0)
    h2 = jnp.concatenate([hp.reshape(n_p, D_MODEL), hs.reshape(n_s, D_MODEL)], axis=0)
    x2 = jnp.pad(x2, ((0, t_pad - n_p - n_s), (0, 0)))
    h2t = jnp.pad(h2, ((0, t_pad - n_p - n_s), (0, 0))).T.astype(BF16)
    y = _peer(x2, h2t, w_pq[0], sub_keys[0], expert_u[0], expert_v[0])
    yp = y[:n_p].reshape(xp.shape)
    ys = y[n_p:n_p + n_s].reshape(xs.shape)
    st = lambda a: a[None]
    return (yp, ys, st(k1), st(v1), st(ik1), st(p1), st(k2), st(v2), st(ik2), st(p2))
```

```python
import functools
import math

import jax
import jax.numpy as jnp
import numpy as np
from jax import lax
from jax.experimental import pallas as pl
from jax.experimental.pallas import tpu as pltpu

F32 = jnp.float32
BF16 = jnp.bfloat16
I32 = jnp.int32

LANES = 128
SUBLANES = 8
VMEM_LIMIT_BYTES = 56 * 1024 * 1024

D_MODEL = 1024
POOL_W = 512
POOL_WINDOWS = (2, 4, 8, 16)
POOL_GW = 128
POOL_STATE = 15
HEAD_DIM = 128
N_HEADS = 4
IDX_HEADS = 8
IDX_DIM = 64
TOPK_MAX = 256
ROPE_THETA = 10000.0
PAGE_SIZE = 128
N_KEYS = 128
PEER_HEADS = 8
PEER_TOPK = 16
PEER_HALF = 128
EPS = 1e-6
NEG_INF = float("-inf")


def _cparams(sem):
    return pltpu.CompilerParams(dimension_semantics=sem, vmem_limit_bytes=VMEM_LIMIT_BYTES)


PEER_TT = 128
CAND_ROWS = 16 + 8 * 15


def _top16_rows(x, iota0):
    vals, idxs = [], []
    for _ in range(PEER_TOPK):
        m = jnp.max(x, axis=0, keepdims=True)
        idx = jnp.min(jnp.where(x == m, iota0, N_KEYS), axis=0, keepdims=True)
        x = jnp.where(iota0 == idx, NEG_INF, x)
        vals.append(m)
        idxs.append(idx)
    return vals, idxs


def _peer_select_kernel(h2t_ref, wpqt_ref, sk_ref, n1_ref, w1_ref, r2_ref, p2_ref, s_scr):
    tt = h2t_ref.shape[1]
    qt = jnp.dot(wpqt_ref[...], h2t_ref[...], preferred_element_type=F32).astype(BF16)
    for hc in range(2 * PEER_HEADS):
        s_scr[hc] = jnp.dot(sk_ref[hc], qt[hc * PEER_HALF:(hc + 1) * PEER_HALF, :],
                            preferred_element_type=F32)

    iota0 = lax.broadcasted_iota(I32, (N_KEYS, tt), 0)
    rows = lax.broadcasted_iota(I32, (CAND_ROWS, tt), 0)
    ca = jnp.where(rows < 16, 0, ((rows - 16) >> 3) + 1)
    cb = jnp.where(rows < 16, rows, (rows - 16) & 7)
    cvalid = (ca + 1) * (cb + 1) <= PEER_TOPK
    cflat = ca * PEER_TOPK + cb

    def head(h, carry):
        sv1, si1 = _top16_rows(s_scr[2 * h], iota0)
        x2 = s_scr[2 * h + 1]
        r2 = jnp.full((N_KEYS, tt), float(N_KEYS - 1), F32)
        p2 = jnp.zeros((N_KEYS, tt), F32)
        sv2 = []
        for r in range(PEER_TOPK):
            m = jnp.max(x2, axis=0, keepdims=True)
            idx = jnp.min(jnp.where(x2 == m, iota0, N_KEYS), axis=0, keepdims=True)
            hit = iota0 == idx
            x2 = jnp.where(hit, NEG_INF, x2)
            sv2.append(m)
            r2 = jnp.where(hit, float(r), r2)
            p2 = jnp.where(hit, jnp.exp(m - sv2[0]), p2)
        e1 = [jnp.exp(v - sv1[0]) for v in sv1]
        sv2a = jnp.concatenate(sv2, axis=0)
        e2a = jnp.exp(sv2a - sv2[0])
        cand = jnp.concatenate([sv1[0] + sv2a] + [sv1[a] + sv2a[:8] for a in range(1, 16)], axis=0)
        egrid = jnp.concatenate([e1[0] * e2a] + [e1[a] * e2a[:8] for a in range(1, 16)], axis=0)
        cand = jnp.where(cvalid, cand, NEG_INF)
        sel = jnp.zeros((CAND_ROWS, tt), F32)
        for _ in range(PEER_TOPK):
            m = jnp.max(cand, axis=0, keepdims=True)
            f = jnp.min(jnp.where(cand == m, cflat, PEER_TOPK * PEER_TOPK), axis=0, keepdims=True)
            hit = cflat == f
            cand = jnp.where(hit, NEG_INF, cand)
            sel = jnp.where(hit, 1.0, sel)
        z = jnp.sum(sel * egrid, axis=0, keepdims=True)
        inv_z = 1.0 / z
        n1 = jnp.zeros((N_KEYS, tt), F32)
        w1 = jnp.zeros((N_KEYS, tt), F32)
        for a in range(PEER_TOPK):
            lo = 0 if a == 0 else 16 + 8 * (a - 1)
            hi = 16 if a == 0 else lo + 8
            n_a = jnp.sum(sel[lo:hi], axis=0, keepdims=True)
            hit = iota0 == si1[a]
            n1 = jnp.where(hit, n_a, n1)
            w1 = jnp.where(hit, e1[a] * inv_z, w1)
        n1_ref[h] = n1
        w1_ref[h] = w1
        r2_ref[h] = r2.astype(BF16)
        p2_ref[h] = p2.astype(BF16)
        return carry

    lax.fori_loop(0, PEER_HEADS, head, 0)


def _peer_select(h2t, wpqt, sk):
    t = h2t.shape[1]
    tt = PEER_TT
    assert t % tt == 0
    tok = lambda i: (0, 0, i)
    shp = (PEER_HEADS, N_KEYS, t)
    return pl.pallas_call(
        _peer_select_kernel,
        grid=(t // tt,),
        in_specs=[
            pl.BlockSpec((D_MODEL, tt), lambda i: (0, i)),
            pl.BlockSpec(wpqt.shape, lambda i: (0, 0)),
            pl.BlockSpec(sk.shape, lambda i: (0, 0, 0)),
        ],
        out_specs=[pl.BlockSpec((PEER_HEADS, N_KEYS, tt), tok)] * 4,
        out_shape=[jax.ShapeDtypeStruct(shp, F32), jax.ShapeDtypeStruct(shp, F32),
                   jax.ShapeDtypeStruct(shp, BF16), jax.ShapeDtypeStruct(shp, BF16)],
        scratch_shapes=[pltpu.VMEM((2 * PEER_HEADS, N_KEYS, tt), F32)],
        compiler_params=_cparams(("arbitrary",)),
        name="peer_select",
    )(h2t, wpqt, sk)


PEER_TM = 512
PEER_TE = 1024


def _gelu_exact(x):
    return 0.5 * x * (1.0 + lax.erf(x * (1.0 / math.sqrt(2.0))))


def _peer_expert_kernel(xt_ref, u_ref, vt_ref, n1_ref, w1_ref, r2_ref, p2_ref, x2_ref, out_ref,
                        acc_ref, m_ref):
    j = pl.program_id(1)
    nc = PEER_TE // N_KEYS
    tm = xt_ref.shape[1]

    @pl.when(j == 0)
    def _():
        acc_ref[...] = jnp.zeros_like(acc_ref)

    s = jnp.dot(u_ref[...], xt_ref[...], preferred_element_type=F32)
    c0 = pl.multiple_of(j * nc, nc)
    n_rows = [n1_ref[h, pl.ds(c0, nc), :].astype(BF16) for h in range(PEER_HEADS)]
    w_rows = [w1_ref[h, pl.ds(c0, nc), :].astype(BF16) for h in range(PEER_HEADS)]
    for cc in range(nc):
        a = _gelu_exact(s[cc * N_KEYS:(cc + 1) * N_KEYS, :])
        g = jnp.zeros((N_KEYS, tm), BF16)
        for h in range(PEER_HEADS):
            nb = jnp.broadcast_to(n_rows[h][cc:cc + 1, :], (N_KEYS, tm))
            wb = jnp.broadcast_to(w_rows[h][cc:cc + 1, :], (N_KEYS, tm))
            g = g + jnp.where(r2_ref[h] < nb, p2_ref[h], jnp.zeros((), BF16)) * wb
        m_ref[cc * N_KEYS:(cc + 1) * N_KEYS, :] = a.astype(BF16) * g
    acc_ref[...] += jnp.dot(vt_ref[...], m_ref[...], preferred_element_type=F32)

    @pl.when(j == pl.num_programs(1) - 1)
    def _():
        out_ref[...] = x2_ref[...] + acc_ref[...].T


def _peer_experts(xt, u, vt, n1, w1, r2, p2, x2):
    d, t = xt.shape
    e = u.shape[0]
    tm, te = PEER_TM, PEER_TE
    assert t % tm == 0 and e % te == 0
    fac = pl.BlockSpec((PEER_HEADS, N_KEYS, tm), lambda i, j: (0, 0, i))
    return pl.pallas_call(
        _peer_expert_kernel,
        grid=(t // tm, e // te),
        in_specs=[
            pl.BlockSpec((d, tm), lambda i, j: (0, i)),
            pl.BlockSpec((te, d), lambda i, j: (j, 0)),
            pl.BlockSpec((d, te), lambda i, j: (0, j)),
            fac, fac, fac, fac,
            pl.BlockSpec((tm, d), lambda i, j: (i, 0)),
        ],
        out_specs=pl.BlockSpec((tm, d), lambda i, j: (i, 0)),
        out_shape=jax.ShapeDtypeStruct((t, d), F32),
        scratch_shapes=[pltpu.VMEM((d, tm), F32), pltpu.VMEM((te, tm), BF16)],
        compiler_params=_cparams(("arbitrary", "arbitrary")),
        name="peer_experts",
    )(xt, u, vt, n1, w1, r2, p2, x2)


def _peer(x2, h2t, w_pq, sub_keys, expert_u, expert_v):
    wpqt = w_pq.T.astype(BF16)
    sk = sub_keys.reshape(2 * PEER_HEADS, N_KEYS, PEER_HALF).astype(BF16)
    n1, w1, r2, p2 = _peer_select(h2t, wpqt, sk)
    return _peer_experts(h2t, expert_u.astype(BF16), expert_v.T.astype(BF16), n1, w1, r2, p2, x2)


DSA_TQ = 256
INT_MIN = -2 ** 31


def _order_key(score):
    b = pltpu.bitcast(score + 0.0, I32)
    return jnp.where(b < 0, b ^ 0x7FFFFFFF, b)


def _kth_largest_key(count_ge, k, shape):
    def bit_step(i, t_u):
        cand_u = t_u | (jnp.int32(1) << (31 - i))
        cnt = count_ge(cand_u ^ INT_MIN)
        return jnp.where(cnt >= k, cand_u, t_u)
    t_u = lax.fori_loop(0, 32, bit_step, jnp.zeros(shape, I32))
    return t_u ^ INT_MIN


def _dsa_prompt_kernel(iq_ref, iw_ref, q_ref, ik_ref, k_ref, v_ref, o_ref,
                       key_scr, tie_scr, m_scr, l_scr, acc_scr, carry_scr, *, k_sel):
    tq = DSA_TQ
    qb = pl.program_id(1)
    n_chunks = qb + 1
    row = lax.broadcasted_iota(I32, (tq, tq), 0)
    col = lax.broadcasted_iota(I32, (tq, tq), 1)
    nt = (((1,), (1,)), ((), ()))

    def chunk_start(j):
        return pl.multiple_of(j * tq, tq)

    def score_chunk(j, c):
        ikc = ik_ref[0, pl.ds(chunk_start(j), tq), :]
        sc = jnp.zeros((tq, tq), F32)
        for h in range(IDX_HEADS):
            s = lax.dot_general(iq_ref[0, :, h * IDX_DIM:(h + 1) * IDX_DIM], ikc, nt,
                                preferred_element_type=F32)
            sc = sc + jnp.maximum(s, 0.0) * iw_ref[0, :, h:h + 1]
        key = _order_key(sc)
        key = jnp.where((j < qb) | (col <= row), key, INT_MIN)
        key_scr[:, pl.ds(chunk_start(j), tq)] = key
        return c

    lax.fori_loop(0, n_chunks, score_chunk, 0)

    def count(pred):
        def body(j, acc):
            kc = key_scr[:, pl.ds(chunk_start(j), tq)]
            m = jnp.where(pred(kc), 1, 0)
            return acc + m[:, :LANES] + m[:, LANES:]
        acc = lax.fori_loop(0, n_chunks, body, jnp.zeros((tq, LANES), I32))
        return jnp.sum(acc, axis=1, keepdims=True)

    thr = _kth_largest_key(lambda t: count(lambda kc: kc >= t), k_sel, (tq, 1))
    n_gt = count(lambda kc: kc > thr)
    n_eq = count(lambda kc: (kc == thr) & (kc > INT_MIN))
    room = k_sel - n_gt
    ties_cut = jnp.max(n_eq - room) > 0

    m_scr[...] = jnp.full(m_scr.shape, NEG_INF, F32)
    l_scr[...] = jnp.zeros(l_scr.shape, F32)
    acc_scr[...] = jnp.zeros(acc_scr.shape, F32)
    carry_scr[...] = jnp.zeros(carry_scr.shape, F32)

    def attend_chunk(j, c):
        kc = key_scr[:, pl.ds(chunk_start(j), tq)]
        eq = (kc == thr) & (kc > INT_MIN)

        @pl.when(ties_cut)
        def _():
            tri = jnp.where(row <= col, 1.0, 0.0).astype(BF16)
            eqf = jnp.where(eq, 1.0, 0.0)
            prefix = jnp.dot(eqf.astype(BF16), tri, preferred_element_type=F32) + carry_scr[...]
            tie_scr[...] = jnp.where(prefix <= room.astype(F32), 1.0, 0.0)
            carry_scr[...] = carry_scr[...] + jnp.sum(eqf, axis=1, keepdims=True)

        @pl.when(jnp.logical_not(ties_cut))
        def _():
            tie_scr[...] = jnp.ones(tie_scr.shape, F32)

        sel = (kc > thr) | (eq & (tie_scr[...] > 0.0))
        kch = k_ref[0, pl.ds(chunk_start(j), tq), :]
        vch = v_ref[0, pl.ds(chunk_start(j), tq), :]
        for h in range(N_HEADS):
            qh = q_ref[0, :, h * HEAD_DIM:(h + 1) * HEAD_DIM]
            lg = lax.dot_general(qh, kch, nt, preferred_element_type=F32) * (HEAD_DIM ** -0.5)
            lg = jnp.where(sel, lg, NEG_INF)
            m_old = m_scr[h]
            m_new = jnp.maximum(m_old, jnp.max(lg, axis=1, keepdims=True))
            m_safe = jnp.where(m_new == NEG_INF, 0.0, m_new)
            alpha = jnp.exp(m_old - m_safe)
            p = jnp.exp(lg - m_safe)
            l_scr[h] = alpha * l_scr[h] + jnp.sum(p, axis=1, keepdims=True)
            acc_scr[h] = alpha * acc_scr[h] + jnp.dot(p.astype(BF16), vch, preferred_element_type=F32)
            m_scr[h] = m_new
        return c

    lax.fori_loop(0, n_chunks, attend_chunk, 0)
    for h in range(N_HEADS):
        o_ref[0, :, h * HEAD_DIM:(h + 1) * HEAD_DIM] = (acc_scr[h] / l_scr[h]).astype(o_ref.dtype)


def _dsa_prompt_pallas(q, k, v, iq, ik, iw):
    b, s, _ = q.shape
    tq = DSA_TQ
    assert s % tq == 0
    k_sel = min(TOPK_MAX, s // 4)
    blk = lambda w: pl.BlockSpec((1, tq, w), lambda bi, qi: (bi, qi, 0))
    seq = lambda w: pl.BlockSpec((1, s, w), lambda bi, qi: (bi, 0, 0))
    return pl.pallas_call(
        functools.partial(_dsa_prompt_kernel, k_sel=k_sel),
        grid=(b, s // tq),
        in_specs=[blk(IDX_HEADS * IDX_DIM), blk(IDX_HEADS), blk(N_HEADS * HEAD_DIM),
                  seq(IDX_DIM), seq(HEAD_DIM), seq(HEAD_DIM)],
        out_specs=blk(N_HEADS * HEAD_DIM),
        out_shape=jax.ShapeDtypeStruct((b, s, N_HEADS * HEAD_DIM), BF16),
        scratch_shapes=[
            pltpu.VMEM((tq, s), I32),
            pltpu.VMEM((tq, tq), F32),
            pltpu.VMEM((N_HEADS, tq, 1), F32),
            pltpu.VMEM((N_HEADS, tq, 1), F32),
            pltpu.VMEM((N_HEADS, tq, HEAD_DIM), F32),
            pltpu.VMEM((tq, 1), F32),
        ],
        compiler_params=_cparams(("arbitrary", "arbitrary")),
        name="dsa_prompt",
    )(iq, iw, q, ik, k, v)


SPLIT_SIZES = (POOL_W, N_HEADS * HEAD_DIM, HEAD_DIM, HEAD_DIM, IDX_HEADS * IDX_DIM, IDX_DIM, IDX_HEADS)


def _rms_norm(x, g):
    xf = x.astype(F32)
    y = xf * lax.rsqrt(jnp.mean(xf * xf, axis=-1, keepdims=True) + EPS)
    return (y * g.astype(F32)).astype(x.dtype)


def _rope(x, pos):
    half = x.shape[-1] // 2
    inv = ROPE_THETA ** (-jnp.arange(half, dtype=F32) / half)
    ang = pos.astype(F32)[:, None] * inv[None, :]
    cos = jnp.cos(ang)[None, :, None, :]
    sin = jnp.sin(ang)[None, :, None, :]
    x1, x2 = x[..., :half], x[..., half:]
    return jnp.concatenate([x1 * cos - x2 * sin, x2 * cos + x1 * sin], axis=-1)


def _project(h, pos, w_in, q_norm_g, k_norm_g):
    B, T, _ = h.shape
    z = h @ w_in
    offs = np.cumsum(np.array(SPLIT_SIZES))[:-1].tolist()
    u, q, k, v, iq, ik, iw = jnp.split(z, offs, axis=-1)
    q = _rope(_rms_norm(q.reshape(B, T, N_HEADS, HEAD_DIM), q_norm_g), pos)
    k = _rope(_rms_norm(k[:, :, None, :], k_norm_g), pos)[:, :, 0]
    iq = _rope(iq.reshape(B, T, IDX_HEADS, IDX_DIM), pos)
    ik = _rope(ik[:, :, None, :], pos)[:, :, 0]
    iw = iw * (IDX_HEADS ** -0.5)
    return u, q, k, v, iq, ik, iw


def _pool_mix(u, prefix, start_pos, w_pool, pool_scale):
    T = u.shape[1]
    full = jnp.concatenate([prefix, u], axis=1)
    c = jnp.cumsum(full.astype(F32), axis=1)
    c = jnp.pad(c, ((0, 0), (1, 0), (0, 0)))
    pos = start_pos + jnp.arange(T)
    outs = []
    for g, w in enumerate(POOL_WINDOWS):
        sl = slice(g * POOL_GW, (g + 1) * POOL_GW)
        hi = c[:, POOL_STATE + 1:POOL_STATE + 1 + T, sl]
        lo = c[:, POOL_STATE + 1 - w:POOL_STATE + 1 - w + T, sl]
        cnt = jnp.minimum(pos + 1, w).astype(F32)[None, :, None]
        d = ((hi - lo) / cnt - u[..., sl].astype(F32)).astype(u.dtype)
        outs.append(d @ w_pool[g])
    y = jnp.concatenate(outs, axis=-1) * pool_scale
    return y, full[:, -POOL_STATE:]


def _take_rows(a, idx):
    return jax.vmap(lambda ab, ib: ab[ib])(a, idx)


def _dsa_block(q, iq, iw, qpos, idx_keys, k_sel, gather_kv):
    L = idx_keys.shape[1]
    s = jnp.einsum('bqhd,bld->bqhl', iq, idx_keys)
    score = jnp.einsum('bqhl,bqh->bql', jax.nn.relu(s), iw).astype(F32)
    adm = jnp.arange(L)[None, None, :] <= qpos[None, :, None]
    score = jnp.where(adm, score, -jnp.inf)
    top_val, top_idx = lax.top_k(score, k_sel)
    valid = jnp.isfinite(top_val)
    k_rows, v_rows = gather_kv(top_idx)
    logits = jnp.einsum('bqhd,bqkd->bqhk', q, k_rows).astype(F32) * (HEAD_DIM ** -0.5)
    logits = jnp.where(valid[:, :, None, :], logits, -jnp.inf)
    p = jax.nn.softmax(logits, axis=-1).astype(v_rows.dtype)
    return jnp.einsum('bqhk,bqkd->bqhd', p, v_rows)


def _dsa_prompt(q, k, v, iq, ik, iw):
    B, S = q.shape[:2]
    k_sel = min(TOPK_MAX, S // 4)
    QB = 128
    nb = S // QB

    def to_blocks(a):
        return a.reshape(B, nb, QB, *a.shape[2:]).swapaxes(0, 1)

    qpos = jnp.arange(S).reshape(nb, QB)
    gather = lambda idx: (_take_rows(k, idx), _take_rows(v, idx))

    def body(args):
        qb, iqb, iwb, pb = args
        return _dsa_block(qb, iqb, iwb, pb, ik, k_sel, gather)

    out = lax.map(body, (to_blocks(q), to_blocks(iq), to_blocks(iw), qpos))
    return out.swapaxes(0, 1).reshape(B, S, N_HEADS * HEAD_DIM)


def _dsa_sample(q, k, v, iq, ik, iw, cache_k, cache_v, cache_ik, page_table):
    DB, DS = q.shape[:2]
    P = page_table.shape[1] * PAGE_SIZE
    past_ik = cache_ik[page_table].reshape(DB, P, IDX_DIM)
    idx_keys = jnp.concatenate([past_ik, ik], axis=1)
    k_sel = min(TOPK_MAX, (P + DS) // 4)

    def gather(idx):
        in_past = (idx < P)[..., None]
        pidx = jnp.minimum(idx, P - 1)
        phys = jax.vmap(lambda pt, i: pt[i])(page_table, pidx // PAGE_SIZE)
        off = pidx % PAGE_SIZE
        nidx = jnp.clip(idx - P, 0, DS - 1)
        k_rows = jnp.where(in_past, cache_k[phys, off], _take_rows(k, nidx))
        v_rows = jnp.where(in_past, cache_v[phys, off], _take_rows(v, nidx))
        return k_rows, v_rows

    qpos = P + jnp.arange(DS)
    out = _dsa_block(q, iq, iw, qpos, idx_keys, k_sel, gather)
    return out.reshape(DB, DS, N_HEADS * HEAD_DIM)


def _front(x, pos0, pool_prefix, attend, norm1_g, w_in, q_norm_g, k_norm_g, w_pool, pool_scale, w_o,
           norm2_g):
    T = x.shape[1]
    pos = pos0 + jnp.arange(T)
    h = _rms_norm(x, norm1_g)
    u, q, k, v, iq, ik, iw = _project(h, pos, w_in, q_norm_g, k_norm_g)
    ya, pool_state = _pool_mix(u, pool_prefix, pos0, w_pool, pool_scale)
    yb = attend(q, k, v, iq, ik, iw)
    x = x + jnp.concatenate([ya, yb], axis=-1) @ w_o
    h2 = _rms_norm(x, norm2_g)
    return x, h2, k, v, ik, pool_state


def kernel(x_prompt, x_sample, cache_k, cache_v, cache_ik, state_pool, page_table, norm1_g, w_in,
           q_norm_g, k_norm_g, w_pool, pool_scale, w_o, norm2_g, w_pq, sub_keys, expert_u, expert_v):
    past = page_table.shape[1] * PAGE_SIZE
    lw = (norm1_g[0], w_in[0], q_norm_g[0], k_norm_g[0], w_pool[0], pool_scale[0], w_o[0], norm2_g[0])
    prefix = jnp.zeros((x_prompt.shape[0], POOL_STATE, POOL_W), x_prompt.dtype)
    def dsa_p(q, k, v, iq, ik, iw):
        b, s = q.shape[:2]
        return _dsa_prompt_pallas(q.reshape(b, s, -1).astype(BF16), k.astype(BF16), v.astype(BF16),
                                  iq.reshape(b, s, -1).astype(BF16), ik.astype(BF16), iw).astype(F32)

    xp, hp, k1, v1, ik1, p1 = _front(x_prompt, 0, prefix, dsa_p, *lw)
    attend_s = functools.partial(_dsa_sample, cache_k=cache_k[0], cache_v=cache_v[0],
                                 cache_ik=cache_ik[0], page_table=page_table)
    xs, hs, k2, v2, ik2, p2 = _front(x_sample, past, state_pool[0], attend_s, *lw)
    n_p = xp.shape[0] * xp.shape[1]
    n_s = xs.shape[0] * xs.shape[1]
    t_pad = -(-(n_p + n_s) // PEER_TM) * PEER_TM
    x2 = jnp.concatenate([xp.reshape(n_p, D_MODEL), xs.reshape(n_s, D_MODEL)], axis=0)
    h2 = jnp.concatenate([hp.reshape(n_p, D_MODEL), hs.reshape(n_s, D_MODEL)], axis=0)
    x2 = jnp.pad(x2, ((0, t_pad - n_p - n_s), (0, 0)))
    h2t = jnp.pad(h2, ((0, t_pad - n_p - n_s), (0, 0))).T.astype(BF16)
    y = _peer(x2, h2t, w_pq[0], sub_keys[0], expert_u[0], expert_v[0])
    yp = y[:n_p].reshape(xp.shape)
    ys = y[n_p:n_p + n_s].reshape(xs.shape)
    st = lambda a: a[None]
    return (yp, ys, st(k1), st(v1), st(ik1), st(p1), st(k2), st(v2), st(ik2), st(p2))
```

```python
import functools
import math

import jax
import jax.numpy as jnp
import numpy as np
from jax import lax
from jax.experimental import pallas as pl
from jax.experimental.pallas import tpu as pltpu

F32 = jnp.float32
BF16 = jnp.bfloat16
I32 = jnp.int32

LANES = 128
SUBLANES = 8
VMEM_LIMIT_BYTES = 56 * 1024 * 1024

D_MODEL = 1024
POOL_W = 512
POOL_WINDOWS = (2, 4, 8, 16)
POOL_GW = 128
POOL_STATE = 15
HEAD_DIM = 128
N_HEADS = 4
IDX_HEADS = 8
IDX_DIM = 64
TOPK_MAX = 256
ROPE_THETA = 10000.0
PAGE_SIZE = 128
N_KEYS = 128
PEER_HEADS = 8
PEER_TOPK = 16
PEER_HALF = 128
EPS = 1e-6
NEG_INF = float("-inf")


def _cparams(sem):
    return pltpu.CompilerParams(dimension_semantics=sem, vmem_limit_bytes=VMEM_LIMIT_BYTES)


PEER_TT = 128
CAND_ROWS = 16 + 8 * 15


def _top16_rows(x, iota0):
    vals, idxs = [], []
    for _ in range(PEER_TOPK):
        m = jnp.max(x, axis=0, keepdims=True)
        idx = jnp.min(jnp.where(x == m, iota0, N_KEYS), axis=0, keepdims=True)
        x = jnp.where(iota0 == idx, NEG_INF, x)
        vals.append(m)
        idxs.append(idx)
    return vals, idxs


def _peer_select_kernel(h2t_ref, wpqt_ref, sk_ref, n1_ref, w1_ref, r2_ref, p2_ref, s_scr):
    tt = h2t_ref.shape[1]
    qt = jnp.dot(wpqt_ref[...], h2t_ref[...], preferred_element_type=F32).astype(BF16)
    for hc in range(2 * PEER_HEADS):
        s_scr[hc] = jnp.dot(sk_ref[hc], qt[hc * PEER_HALF:(hc + 1) * PEER_HALF, :],
                            preferred_element_type=F32)

    iota0 = lax.broadcasted_iota(I32, (N_KEYS, tt), 0)
    rows = lax.broadcasted_iota(I32, (CAND_ROWS, tt), 0)
    ca = jnp.where(rows < 16, 0, ((rows - 16) >> 3) + 1)
    cb = jnp.where(rows < 16, rows, (rows - 16) & 7)
    cvalid = (ca + 1) * (cb + 1) <= PEER_TOPK
    cflat = ca * PEER_TOPK + cb

    def head(h, carry):
        sv1, si1 = _top16_rows(s_scr[2 * h], iota0)
        x2 = s_scr[2 * h + 1]
        r2 = jnp.full((N_KEYS, tt), float(N_KEYS - 1), F32)
        p2 = jnp.zeros((N_KEYS, tt), F32)
        sv2 = []
        for r in range(PEER_TOPK):
            m = jnp.max(x2, axis=0, keepdims=True)
            idx = jnp.min(jnp.where(x2 == m, iota0, N_KEYS), axis=0, keepdims=True)
            hit = iota0 == idx
            x2 = jnp.where(hit, NEG_INF, x2)
            sv2.append(m)
            r2 = jnp.where(hit, float(r), r2)
            p2 = jnp.where(hit, jnp.exp(m - sv2[0]), p2)
        e1 = [jnp.exp(v - sv1[0]) for v in sv1]
        sv2a = jnp.concatenate(sv2, axis=0)
        e2a = jnp.exp(sv2a - sv2[0])
        cand = jnp.concatenate([sv1[0] + sv2a] + [sv1[a] + sv2a[:8] for a in range(1, 16)], axis=0)
        egrid = jnp.concatenate([e1[0] * e2a] + [e1[a] * e2a[:8] for a in range(1, 16)], axis=0)
        cand = jnp.where(cvalid, cand, NEG_INF)
        sel = jnp.zeros((CAND_ROWS, tt), F32)
        for _ in range(PEER_TOPK):
            m = jnp.max(cand, axis=0, keepdims=True)
            f = jnp.min(jnp.where(cand == m, cflat, PEER_TOPK * PEER_TOPK), axis=0, keepdims=True)
            hit = cflat == f
            cand = jnp.where(hit, NEG_INF, cand)
            sel = jnp.where(hit, 1.0, sel)
        z = jnp.sum(sel * egrid, axis=0, keepdims=True)
        inv_z = 1.0 / z
        n1 = jnp.zeros((N_KEYS, tt), F32)
        w1 = jnp.zeros((N_KEYS, tt), F32)
        for a in range(PEER_TOPK):
            lo = 0 if a == 0 else 16 + 8 * (a - 1)
            hi = 16 if a == 0 else lo + 8
            n_a = jnp.sum(sel[lo:hi], axis=0, keepdims=True)
            hit = iota0 == si1[a]
            n1 = jnp.where(hit, n_a, n1)
            w1 = jnp.where(hit, e1[a] * inv_z, w1)
        n1_ref[h] = n1
        w1_ref[h] = w1
        r2_ref[h] = r2.astype(BF16)
        p2_ref[h] = p2.astype(BF16)
        return carry

    lax.fori_loop(0, PEER_HEADS, head, 0)


def _peer_select(h2t, wpqt, sk):
    t = h2t.shape[1]
    tt = PEER_TT
    assert t % tt == 0
    tok = lambda i: (0, 0, i)
    shp = (PEER_HEADS, N_KEYS, t)
    return pl.pallas_call(
        _peer_select_kernel,
        grid=(t // tt,),
        in_specs=[
            pl.BlockSpec((D_MODEL, tt), lambda i: (0, i)),
            pl.BlockSpec(wpqt.shape, lambda i: (0, 0)),
            pl.BlockSpec(sk.shape, lambda i: (0, 0, 0)),
        ],
        out_specs=[pl.BlockSpec((PEER_HEADS, N_KEYS, tt), tok)] * 4,
        out_shape=[jax.ShapeDtypeStruct(shp, F32), jax.ShapeDtypeStruct(shp, F32),
                   jax.ShapeDtypeStruct(shp, BF16), jax.ShapeDtypeStruct(shp, BF16)],
        scratch_shapes=[pltpu.VMEM((2 * PEER_HEADS, N_KEYS, tt), F32)],
        compiler_params=_cparams(("arbitrary",)),
        name="peer_select",
    )(h2t, wpqt, sk)


PEER_TM = 512
PEER_TE = 1024


def _gelu_exact(x):
    return 0.5 * x * (1.0 + lax.erf(x * (1.0 / math.sqrt(2.0))))


def _peer_expert_kernel(xt_ref, u_ref, vt_ref, n1_ref, w1_ref, r2_ref, p2_ref, x2_ref, out_ref,
                        acc_ref, m_ref):
    j = pl.program_id(1)
    nc = PEER_TE // N_KEYS
    tm = xt_ref.shape[1]

    @pl.when(j == 0)
    def _():
        acc_ref[...] = jnp.zeros_like(acc_ref)

    s = jnp.dot(u_ref[...], xt_ref[...], preferred_element_type=F32)
    c0 = pl.multiple_of(j * nc, nc)
    n_rows = [n1_ref[h, pl.ds(c0, nc), :].astype(BF16) for h in range(PEER_HEADS)]
    w_rows = [w1_ref[h, pl.ds(c0, nc), :].astype(BF16) for h in range(PEER_HEADS)]
    for cc in range(nc):
        a = _gelu_exact(s[cc * N_KEYS:(cc + 1) * N_KEYS, :])
        g = jnp.zeros((N_KEYS, tm), BF16)
        for h in range(PEER_HEADS):
            nb = jnp.broadcast_to(n_rows[h][cc:cc + 1, :], (N_KEYS, tm))
            wb = jnp.broadcast_to(w_rows[h][cc:cc + 1, :], (N_KEYS, tm))
            g = g + jnp.where(r2_ref[h] < nb, p2_ref[h], jnp.zeros((), BF16)) * wb
        m_ref[cc * N_KEYS:(cc + 1) * N_KEYS, :] = a.astype(BF16) * g
    acc_ref[...] += jnp.dot(vt_ref[...], m_ref[...], preferred_element_type=F32)

    @pl.when(j == pl.num_programs(1) - 1)
    def _():
        out_ref[...] = x2_ref[...] + acc_ref[...].T


def _peer_experts(xt, u, vt, n1, w1, r2, p2, x2):
    d, t = xt.shape
    e = u.shape[0]
    tm, te = PEER_TM, PEER_TE
    assert t % tm == 0 and e % te == 0
    fac = pl.BlockSpec((PEER_HEADS, N_KEYS, tm), lambda i, j: (0, 0, i))
    return pl.pallas_call(
        _peer_expert_kernel,
        grid=(t // tm, e // te),
        in_specs=[
            pl.BlockSpec((d, tm), lambda i, j: (0, i)),
            pl.BlockSpec((te, d), lambda i, j: (j, 0)),
            pl.BlockSpec((d, te), lambda i, j: (0, j)),
            fac, fac, fac, fac,
            pl.BlockSpec((tm, d), lambda i, j: (i, 0)),
        ],
        out_specs=pl.BlockSpec((tm, d), lambda i, j: (i, 0)),
        out_shape=jax.ShapeDtypeStruct((t, d), F32),
        scratch_shapes=[pltpu.VMEM((d, tm), F32), pltpu.VMEM((te, tm), BF16)],
        compiler_params=_cparams(("arbitrary", "arbitrary")),
        name="peer_experts",
    )(xt, u, vt, n1, w1, r2, p2, x2)


def _peer(x2, h2t, w_pq, sub_keys, expert_u, expert_v):
    wpqt = w_pq.T.astype(BF16)
    sk = sub_keys.reshape(2 * PEER_HEADS, N_KEYS, PEER_HALF).astype(BF16)
    n1, w1, r2, p2 = _peer_select(h2t, wpqt, sk)
    return _peer_experts(h2t, expert_u.astype(BF16), expert_v.T.astype(BF16), n1, w1, r2, p2, x2)


IN_TM = 256
C_U, C_Q, C_K, C_V, C_IQ, C_IK, C_IW, C_END = 0, 512, 1024, 1152, 1280, 1792, 1920, 2048
POOL_HALO = 16


def _pack_w_in(w_in):
    o = np.cumsum((0, POOL_W, N_HEADS * HEAD_DIM, HEAD_DIM, HEAD_DIM, IDX_HEADS * IDX_DIM, IDX_DIM, IDX_HEADS))
    z = lambda n: jnp.zeros((D_MODEL, n), w_in.dtype)
    return jnp.concatenate([w_in[:, o[0]:o[5]], w_in[:, o[5]:o[6]], z(C_IW - C_IK - IDX_DIM),
                            w_in[:, o[6]:o[7]], z(C_END - C_IW - IDX_HEADS)], axis=1).astype(BF16)


def _rope_tables(pos):
    def tab(half, reps):
        inv = ROPE_THETA ** (-jnp.arange(half, dtype=F32) / half)
        ang = pos.astype(F32)[:, None] * inv[None, :]
        c, s = jnp.cos(ang), jnp.sin(ang)
        return jnp.tile(jnp.concatenate([c, c], 1), (1, reps)), jnp.tile(jnp.concatenate([-s, s], 1), (1, reps))
    c128, s128 = tab(HEAD_DIM // 2, 1)
    c64, s64 = tab(IDX_DIM // 2, 2)
    return jnp.concatenate([c128, s128, c64, s64], axis=1)


def _rope128(x, tab):
    return x * tab[:, 0:128] + pltpu.roll(x, 64, 1) * tab[:, 128:256]


def _rope64x2(x, tab):
    lane = lax.broadcasted_iota(I32, x.shape, 1)
    partner = jnp.where((lane & 63) < 32, pltpu.roll(x, 96, 1), pltpu.roll(x, 32, 1))
    return x * tab[:, 256:384] + partner * tab[:, 384:512]


def _head_norm(x, g):
    return x * lax.rsqrt(jnp.mean(x * x, axis=-1, keepdims=True) + EPS) * g


def _project_block(x_ref, tab_ref, g1_ref, w_ref, qg_ref, kg_ref,
                   q_ref, k_ref, v_ref, kb_ref, vb_ref, iq_ref, ik_ref, ikb_ref, iw_ref):
    x = x_ref[...]
    h = x * lax.rsqrt(jnp.mean(x * x, axis=-1, keepdims=True) + EPS) * g1_ref[...]
    z = jnp.dot(h.astype(BF16), w_ref[...], preferred_element_type=F32)
    tab = tab_ref[...]
    for hh in range(N_HEADS):
        sl = slice(C_Q + hh * HEAD_DIM, C_Q + (hh + 1) * HEAD_DIM)
        q_ref[:, hh * HEAD_DIM:(hh + 1) * HEAD_DIM] = _rope128(_head_norm(z[:, sl], qg_ref[...]), tab).astype(BF16)
    k = _rope128(_head_norm(z[:, C_K:C_V], kg_ref[...]), tab)
    v = z[:, C_V:C_IQ]
    k_ref[...] = k
    v_ref[...] = v
    kb_ref[...] = k.astype(BF16)
    vb_ref[...] = v.astype(BF16)
    for hp in range(IDX_HEADS // 2):
        sl = slice(C_IQ + hp * 128, C_IQ + (hp + 1) * 128)
        iq_ref[:, hp * 128:(hp + 1) * 128] = _rope64x2(z[:, sl], tab).astype(BF16)
    ik = _rope64x2(z[:, C_IK:C_IW], tab)[:, :IDX_DIM]
    ik_ref[...] = ik
    ikb_ref[...] = ik.astype(BF16)
    iw_ref[...] = z[:, C_IW:C_IW + IDX_HEADS] * (IDX_HEADS ** -0.5)
    return z[:, C_U:C_Q]


def _pool_out(d_list, wp_ref, ps_ref, ya_ref, rows):
    for g in range(len(POOL_WINDOWS)):
        y = jnp.dot(d_list[g].astype(BF16), wp_ref[g], preferred_element_type=F32)
        ya_ref[rows, g * POOL_GW:(g + 1) * POOL_GW] = (y * ps_ref[:, g * POOL_GW:(g + 1) * POOL_GW]).astype(BF16)


def _inproj_prompt_kernel(x_ref, tab_ref, g1_ref, w_ref, qg_ref, kg_ref, wp_ref, ps_ref,
                          ya_ref, q_ref, k_ref, v_ref, kb_ref, vb_ref, iq_ref, ik_ref, ikb_ref, iw_ref,
                          pool_ref, ext_scr):
    tm = x_ref.shape[0]
    blk = pl.program_id(1)

    @pl.when(blk == 0)
    def _():
        ext_scr[0:POOL_HALO, :] = jnp.zeros((POOL_HALO, POOL_W), F32)

    u = _project_block(x_ref, tab_ref, g1_ref, w_ref, qg_ref, kg_ref,
                       q_ref, k_ref, v_ref, kb_ref, vb_ref, iq_ref, ik_ref, ikb_ref, iw_ref)
    ext_scr[POOL_HALO:POOL_HALO + tm, :] = u
    pos = blk * tm + lax.broadcasted_iota(I32, (tm, 1), 0)
    d_list = []
    for g, w in enumerate(POOL_WINDOWS):
        sl = slice(g * POOL_GW, (g + 1) * POOL_GW)
        acc = u[:, sl]
        for j in range(1, w):
            acc = acc + ext_scr[POOL_HALO - j:POOL_HALO - j + tm, sl]
        cnt = jnp.minimum(pos + 1, w).astype(F32)
        d_list.append(acc / cnt - u[:, sl])
    _pool_out(d_list, wp_ref, ps_ref, ya_ref, slice(None))
    tail = ext_scr[tm:tm + POOL_HALO, :]
    pool_ref[...] = tail
    ext_scr[0:POOL_HALO, :] = tail


def _inproj_outs(n, lead):
    sd = lambda w, dt: jax.ShapeDtypeStruct(lead + (n, w), dt)
    return [sd(POOL_W, BF16), sd(N_HEADS * HEAD_DIM, BF16), sd(HEAD_DIM, F32), sd(HEAD_DIM, F32),
            sd(HEAD_DIM, BF16), sd(HEAD_DIM, BF16), sd(IDX_HEADS * IDX_DIM, BF16), sd(IDX_DIM, F32),
            sd(IDX_DIM, BF16), sd(IDX_HEADS, F32)]


def _inproj_prompt(x, tab, g1, w_p, qg, kg, wp, ps):
    b, s, d = x.shape
    tm = IN_TM
    assert s % tm == 0
    full = lambda a: pl.BlockSpec(a.shape, lambda bi, i: (0,) * a.ndim)
    rows = lambda w: pl.BlockSpec((None, tm, w), lambda bi, i: (bi, i, 0))
    outs = _inproj_outs(s, (b,)) + [jax.ShapeDtypeStruct((b, POOL_HALO, POOL_W), F32)]
    out_specs = [rows(o.shape[-1]) for o in outs[:-1]] + [
        pl.BlockSpec((None, POOL_HALO, POOL_W), lambda bi, i: (bi, 0, 0))]
    return pl.pallas_call(
        _inproj_prompt_kernel,
        grid=(b, s // tm),
        in_specs=[rows(d), pl.BlockSpec((tm, tab.shape[1]), lambda bi, i: (i, 0)),
                  full(g1), full(w_p), full(qg), full(kg), full(wp), full(ps)],
        out_specs=out_specs,
        out_shape=outs,
        scratch_shapes=[pltpu.VMEM((POOL_HALO + tm, POOL_W), F32)],
        compiler_params=_cparams(("arbitrary", "arbitrary")),
        name="inproj_prompt",
    )(x, tab, g1, w_p, qg, kg, wp, ps)


def _inproj_sample_kernel(x_ref, tab_ref, g1_ref, w_ref, qg_ref, kg_ref, wp_ref, ps_ref, st_ref,
                          ya_ref, q_ref, k_ref, v_ref, kb_ref, vb_ref, iq_ref, ik_ref, ikb_ref, iw_ref,
                          pool_ref, *, n_seq, n_new, past):
    u = _project_block(x_ref, tab_ref, g1_ref, w_ref, qg_ref, kg_ref,
                       q_ref, k_ref, v_ref, kb_ref, vb_ref, iq_ref, ik_ref, ikb_ref, iw_ref)
    hist = [st_ref[i] for i in range(POOL_STATE)] + [u[j * n_seq:(j + 1) * n_seq, :] for j in range(n_new)]
    for j in range(n_new):
        d_list = []
        for g, w in enumerate(POOL_WINDOWS):
            sl = slice(g * POOL_GW, (g + 1) * POOL_GW)
            acc = hist[POOL_STATE + j][:, sl]
            for i in range(1, w):
                acc = acc + hist[POOL_STATE + j - i][:, sl]
            d_list.append(acc / float(min(past + j + 1, w)) - hist[POOL_STATE + j][:, sl])
        _pool_out(d_list, wp_ref, ps_ref, ya_ref, slice(j * n_seq, (j + 1) * n_seq))
    for i in range(POOL_STATE):
        pool_ref[i] = hist[n_new + i]


def _inproj_sample(x_tm, tab, g1, w_p, qg, kg, wp, ps, st_tm, n_seq, n_new, past):
    n = x_tm.shape[0]
    outs = _inproj_outs(n, ()) + [jax.ShapeDtypeStruct((POOL_STATE, n_seq, POOL_W), F32)]
    return pl.pallas_call(
        functools.partial(_inproj_sample_kernel, n_seq=n_seq, n_new=n_new, past=past),
        out_shape=outs,
        compiler_params=pltpu.CompilerParams(vmem_limit_bytes=VMEM_LIMIT_BYTES),
        name="inproj_sample",
    )(x_tm, tab, g1, w_p, qg, kg, wp, ps, st_tm)


OUT_TM = 512


def _outproj_kernel(x_ref, ya_ref, yb_ref, wa_ref, wb_ref, g2_ref, x2_ref, h2t_ref):
    x2 = (x_ref[...] + jnp.dot(ya_ref[...], wa_ref[...], preferred_element_type=F32)
          + jnp.dot(yb_ref[...], wb_ref[...], preferred_element_type=F32))
    x2_ref[...] = x2
    h2 = x2 * lax.rsqrt(jnp.mean(x2 * x2, axis=-1, keepdims=True) + EPS) * g2_ref[...]
    h2t_ref[...] = h2.T.astype(BF16)


def _outproj(x, ya, yb, w_o, g2):
    t, d = x.shape
    tm = OUT_TM
    assert t % tm == 0
    wa = w_o[:POOL_W].astype(BF16)
    wb = w_o[POOL_W:].astype(BF16)
    rows = lambda w: pl.BlockSpec((tm, w), lambda i: (i, 0))
    full = lambda a: pl.BlockSpec(a.shape, lambda i: (0,) * a.ndim)
    return pl.pallas_call(
        _outproj_kernel,
        grid=(t // tm,),
        in_specs=[rows(d), rows(POOL_W), rows(yb.shape[1]), full(wa), full(wb), full(g2)],
        out_specs=[rows(d), pl.BlockSpec((d, tm), lambda i: (0, i))],
        out_shape=[jax.ShapeDtypeStruct((t, d), F32), jax.ShapeDtypeStruct((d, t), BF16)],
        compiler_params=_cparams(("arbitrary",)),
        name="outproj",
    )(x, ya, yb, wa, wb, g2)


DSA_TQ = 256
INT_MIN = -2 ** 31


def _order_key(score):
    b = pltpu.bitcast(score + 0.0, I32)
    return jnp.where(b < 0, b ^ 0x7FFFFFFF, b)


def _kth_largest_key(count_ge, k, shape):
    def bit_step(i, t_u):
        cand_u = t_u | (jnp.int32(1) << (31 - i))
        cnt = count_ge(cand_u ^ INT_MIN)
        return jnp.where(cnt >= k, cand_u, t_u)
    t_u = lax.fori_loop(0, 32, bit_step, jnp.zeros(shape, I32))
    return t_u ^ INT_MIN


def _dsa_prompt_kernel(iq_ref, iw_ref, q_ref, ik_ref, k_ref, v_ref, o_ref,
                       key_scr, tie_scr, m_scr, l_scr, acc_scr, carry_scr, *, k_sel):
    tq = DSA_TQ
    qb = pl.program_id(1)
    n_chunks = qb + 1
    row = lax.broadcasted_iota(I32, (tq, tq), 0)
    col = lax.broadcasted_iota(I32, (tq, tq), 1)
    nt = (((1,), (1,)), ((), ()))

    def chunk_start(j):
        return pl.multiple_of(j * tq, tq)

    def score_chunk(j, c):
        ikc = ik_ref[0, pl.ds(chunk_start(j), tq), :]
        sc = jnp.zeros((tq, tq), F32)
        for h in range(IDX_HEADS):
            s = lax.dot_general(iq_ref[0, :, h * IDX_DIM:(h + 1) * IDX_DIM], ikc, nt,
                                preferred_element_type=F32)
            sc = sc + jnp.maximum(s, 0.0) * iw_ref[0, :, h:h + 1]
        key = _order_key(sc)
        key = jnp.where((j < qb) | (col <= row), key, INT_MIN)
        key_scr[:, pl.ds(chunk_start(j), tq)] = key
        return c

    lax.fori_loop(0, n_chunks, score_chunk, 0)

    def count(pred):
        def body(j, acc):
            kc = key_scr[:, pl.ds(chunk_start(j), tq)]
            m = jnp.where(pred(kc), 1, 0)
            return acc + m[:, :LANES] + m[:, LANES:]
        acc = lax.fori_loop(0, n_chunks, body, jnp.zeros((tq, LANES), I32))
        return jnp.sum(acc, axis=1, keepdims=True)

    thr = _kth_largest_key(lambda t: count(lambda kc: kc >= t), k_sel, (tq, 1))
    n_gt = count(lambda kc: kc > thr)
    n_eq = count(lambda kc: (kc == thr) & (kc > INT_MIN))
    room = k_sel - n_gt
    ties_cut = jnp.max(n_eq - room) > 0

    m_scr[...] = jnp.full(m_scr.shape, NEG_INF, F32)
    l_scr[...] = jnp.zeros(l_scr.shape, F32)
    acc_scr[...] = jnp.zeros(acc_scr.shape, F32)
    carry_scr[...] = jnp.zeros(carry_scr.shape, F32)

    def attend_chunk(j, c):
        kc = key_scr[:, pl.ds(chunk_start(j), tq)]
        eq = (kc == thr) & (kc > INT_MIN)

        @pl.when(ties_cut)
        def _():
            tri = jnp.where(row <= col, 1.0, 0.0).astype(BF16)
            eqf = jnp.where(eq, 1.0, 0.0)
            prefix = jnp.dot(eqf.astype(BF16), tri, preferred_element_type=F32) + carry_scr[...]
            tie_scr[...] = jnp.where(prefix <= room.astype(F32), 1.0, 0.0)
            carry_scr[...] = carry_scr[...] + jnp.sum(eqf, axis=1, keepdims=True)

        @pl.when(jnp.logical_not(ties_cut))
        def _():
            tie_scr[...] = jnp.ones(tie_scr.shape, F32)

        sel = (kc > thr) | (eq & (tie_scr[...] > 0.0))
        kch = k_ref[0, pl.ds(chunk_start(j), tq), :]
        vch = v_ref[0, pl.ds(chunk_start(j), tq), :]
        for h in range(N_HEADS):
            qh = q_ref[0, :, h * HEAD_DIM:(h + 1) * HEAD_DIM]
            lg = lax.dot_general(qh, kch, nt, preferred_element_type=F32) * (HEAD_DIM ** -0.5)
            lg = jnp.where(sel, lg, NEG_INF)
            m_old = m_scr[h]
            m_new = jnp.maximum(m_old, jnp.max(lg, axis=1, keepdims=True))
            m_safe = jnp.where(m_new == NEG_INF, 0.0, m_new)
            alpha = jnp.exp(m_old - m_safe)
            p = jnp.exp(lg - m_safe)
            l_scr[h] = alpha * l_scr[h] + jnp.sum(p, axis=1, keepdims=True)
            acc_scr[h] = alpha * acc_scr[h] + jnp.dot(p.astype(BF16), vch, preferred_element_type=F32)
            m_scr[h] = m_new
        return c

    lax.fori_loop(0, n_chunks, attend_chunk, 0)
    for h in range(N_HEADS):
        o_ref[0, :, h * HEAD_DIM:(h + 1) * HEAD_DIM] = (acc_scr[h] / l_scr[h]).astype(o_ref.dtype)


def _dsa_prompt_pallas(q, k, v, iq, ik, iw):
    b, s, _ = q.shape
    tq = DSA_TQ
    assert s % tq == 0
    k_sel = min(TOPK_MAX, s // 4)
    blk = lambda w: pl.BlockSpec((1, tq, w), lambda bi, qi: (bi, qi, 0))
    seq = lambda w: pl.BlockSpec((1, s, w), lambda bi, qi: (bi, 0, 0))
    return pl.pallas_call(
        functools.partial(_dsa_prompt_kernel, k_sel=k_sel),
        grid=(b, s // tq),
        in_specs=[blk(IDX_HEADS * IDX_DIM), blk(IDX_HEADS), blk(N_HEADS * HEAD_DIM),
                  seq(IDX_DIM), seq(HEAD_DIM), seq(HEAD_DIM)],
        out_specs=blk(N_HEADS * HEAD_DIM),
        out_shape=jax.ShapeDtypeStruct((b, s, N_HEADS * HEAD_DIM), BF16),
        scratch_shapes=[
            pltpu.VMEM((tq, s), I32),
            pltpu.VMEM((tq, tq), F32),
            pltpu.VMEM((N_HEADS, tq, 1), F32),
            pltpu.VMEM((N_HEADS, tq, 1), F32),
            pltpu.VMEM((N_HEADS, tq, HEAD_DIM), F32),
            pltpu.VMEM((tq, 1), F32),
        ],
        compiler_params=_cparams(("arbitrary", "arbitrary")),
        name="dsa_prompt",
    )(iq, iw, q, ik, k, v)


DEC_ROWS = 8
NEW_PAD = 128


def _dsa_sample_kernel(pt_ref, iq_ref, wsel_ref, q_ref, ikn_ref, kn_ref, vn_ref, cik_ref, ck_ref, cv_ref,
                       o_ref, ikbuf, kbuf, vbuf, sem, tie_scr, *, n_pages, n_new, k_sel):
    bi = pl.program_id(0)
    nb = pl.num_programs(0)
    slot = bi % 2
    n_past = n_pages * PAGE_SIZE
    nt = (((1,), (1,)), ((), ()))

    def page_copies(seq, sl):
        cps = []
        for p in range(n_pages):
            pg = pt_ref[seq, p]
            cps.append(pltpu.make_async_copy(cik_ref.at[pg], ikbuf.at[sl, p], sem.at[sl, 0]))
            cps.append(pltpu.make_async_copy(ck_ref.at[pg], kbuf.at[sl, p], sem.at[sl, 1]))
            cps.append(pltpu.make_async_copy(cv_ref.at[pg], vbuf.at[sl, p], sem.at[sl, 2]))
        return cps

    @pl.when(bi == 0)
    def _():
        for cp in page_copies(0, 0):
            cp.start()

    @pl.when(bi + 1 < nb)
    def _():
        for cp in page_copies(bi + 1, 1 - slot):
            cp.start()

    for cp in page_copies(bi, slot):
        cp.wait()

    iq = iq_ref[...]
    wsel = wsel_ref[...]
    ik_all = ikbuf[slot].reshape(n_past, IDX_DIM).astype(BF16)
    s_past = lax.dot_general(iq, ik_all, nt, preferred_element_type=F32)
    s_new = lax.dot_general(iq, ikn_ref[...], nt, preferred_element_type=F32)
    hi = lax.Precision.HIGHEST
    sc_past = jnp.dot(wsel, jnp.maximum(s_past, 0.0), precision=hi, preferred_element_type=F32)
    sc_new = jnp.dot(wsel, jnp.maximum(s_new, 0.0), precision=hi, preferred_element_type=F32)
    row_p = lax.broadcasted_iota(I32, (DEC_ROWS, n_past), 0)
    row_n = lax.broadcasted_iota(I32, (DEC_ROWS, NEW_PAD), 0)
    col_n = lax.broadcasted_iota(I32, (DEC_ROWS, NEW_PAD), 1)
    key_p = jnp.where(row_p < n_new, _order_key(sc_past), INT_MIN)
    key_n = jnp.where((row_n < n_new) & (col_n <= row_n), _order_key(sc_new), INT_MIN)

    def count(pred):
        one = lambda m: jnp.sum(jnp.where(m, 1, 0), axis=1, keepdims=True)
        return one(pred(key_p)) + one(pred(key_n))

    thr = _kth_largest_key(lambda t: count(lambda kc: kc >= t), k_sel, (DEC_ROWS, 1))
    n_gt = count(lambda kc: kc > thr)
    eq_p = (key_p == thr) & (key_p > INT_MIN)
    eq_n = (key_n == thr) & (key_n > INT_MIN)
    n_eq = (jnp.sum(jnp.where(eq_p, 1, 0), axis=1, keepdims=True)
            + jnp.sum(jnp.where(eq_n, 1, 0), axis=1, keepdims=True))
    room = k_sel - n_gt
    ties_cut = jnp.max(n_eq - room) > 0

    @pl.when(ties_cut)
    def _():
        r = lax.broadcasted_iota(I32, (LANES, LANES), 0)
        c = lax.broadcasted_iota(I32, (LANES, LANES), 1)
        tri = jnp.where(r <= c, 1.0, 0.0).astype(BF16)
        roomf = room.astype(F32)
        carry = jnp.zeros((DEC_ROWS, 1), F32)
        eqf_p = jnp.where(eq_p, 1.0, 0.0)
        for ch in range(n_past // LANES):
            e = eqf_p[:, ch * LANES:(ch + 1) * LANES]
            prefix = jnp.dot(e.astype(BF16), tri, preferred_element_type=F32) + carry
            tie_scr[:, ch * LANES:(ch + 1) * LANES] = jnp.where(prefix <= roomf, 1.0, 0.0)
            carry = carry + jnp.sum(e, axis=1, keepdims=True)
        e = jnp.where(eq_n, 1.0, 0.0)
        prefix = jnp.dot(e.astype(BF16), tri, preferred_element_type=F32) + carry
        tie_scr[:, n_past:n_past + NEW_PAD] = jnp.where(prefix <= roomf, 1.0, 0.0)

    @pl.when(jnp.logical_not(ties_cut))
    def _():
        tie_scr[...] = jnp.ones(tie_scr.shape, F32)

    sel_p = jnp.where((key_p > thr) | (eq_p & (tie_scr[:, 0:n_past] > 0.0)), 1.0, 0.0)
    sel_n = jnp.where((key_n > thr) | (eq_n & (tie_scr[:, n_past:n_past + NEW_PAD] > 0.0)), 1.0, 0.0)
    rr = lax.broadcasted_iota(I32, (N_HEADS * n_new, DEC_ROWS), 0)
    rc = lax.broadcasted_iota(I32, (N_HEADS * n_new, DEC_ROWS), 1)
    hit = rr == rc
    for hh in range(1, N_HEADS):
        hit = hit | (rr == rc + hh * n_new)
    rep = jnp.where(hit & (rc < n_new), 1.0, 0.0).astype(BF16)
    m_p = jnp.dot(rep, sel_p.astype(BF16), preferred_element_type=F32)
    m_n = jnp.dot(rep, sel_n.astype(BF16), preferred_element_type=F32)

    q = q_ref[...]
    scale = HEAD_DIM ** -0.5
    k_all = kbuf[slot].reshape(n_past, HEAD_DIM).astype(BF16)
    lg_p = jnp.where(m_p > 0.5, lax.dot_general(q, k_all, nt, preferred_element_type=F32) * scale, NEG_INF)
    lg_n = jnp.where(m_n > 0.5, lax.dot_general(q, kn_ref[...], nt, preferred_element_type=F32) * scale,
                     NEG_INF)
    mx = jnp.maximum(jnp.max(lg_p, axis=1, keepdims=True), jnp.max(lg_n, axis=1, keepdims=True))
    p_p = jnp.exp(lg_p - mx)
    p_n = jnp.exp(lg_n - mx)
    den = jnp.sum(p_p, axis=1, keepdims=True) + jnp.sum(p_n, axis=1, keepdims=True)
    v_all = vbuf[slot].reshape(n_past, HEAD_DIM).astype(BF16)
    num = (jnp.dot(p_p.astype(BF16), v_all, preferred_element_type=F32)
           + jnp.dot(p_n.astype(BF16), vn_ref[...], preferred_element_type=F32))
    o_ref[...] = num / den


def _dsa_sample_pallas(q, kb, vb, iq, ikb, iw, cache_k, cache_v, cache_ik, page_table):
    db, ds, _ = q.shape
    n_pages = page_table.shape[1]
    n_past = n_pages * PAGE_SIZE
    assert ds <= DEC_ROWS and ds <= NEW_PAD
    k_sel = min(TOPK_MAX, (n_past + ds) // 4)
    heads_first = lambda a, nh: a.reshape(db, ds, nh, -1).transpose(0, 2, 1, 3).reshape(db, nh * ds, -1)
    iq_r = heads_first(iq, IDX_HEADS)
    q_r = heads_first(q, N_HEADS)
    wsel = jnp.einsum('bjh,jk->bjhk', iw, jnp.eye(ds, dtype=F32)).reshape(db, ds, IDX_HEADS * ds)
    wsel = jnp.pad(wsel, ((0, 0), (0, DEC_ROWS - ds), (0, 0)))
    pad_new = lambda a: jnp.pad(a, ((0, 0), (0, NEW_PAD - ds), (0, 0)))
    per_seq = lambda a: pl.BlockSpec((None,) + a.shape[1:], lambda i, pt: (i, 0, 0))
    any_spec = pl.BlockSpec(memory_space=pl.ANY)
    ins = (iq_r, wsel, q_r, pad_new(ikb), pad_new(kb), pad_new(vb))
    out = pl.pallas_call(
        functools.partial(_dsa_sample_kernel, n_pages=n_pages, n_new=ds, k_sel=k_sel),
        grid_spec=pltpu.PrefetchScalarGridSpec(
            num_scalar_prefetch=1,
            grid=(db,),
            in_specs=[per_seq(a) for a in ins] + [any_spec, any_spec, any_spec],
            out_specs=pl.BlockSpec((None, N_HEADS * ds, HEAD_DIM), lambda i, pt: (i, 0, 0)),
            scratch_shapes=[
                pltpu.VMEM((2, n_pages, PAGE_SIZE, IDX_DIM), F32),
                pltpu.VMEM((2, n_pages, PAGE_SIZE, HEAD_DIM), F32),
                pltpu.VMEM((2, n_pages, PAGE_SIZE, HEAD_DIM), F32),
                pltpu.SemaphoreType.DMA((2, 3)),
                pltpu.VMEM((DEC_ROWS, n_past + NEW_PAD), F32),
            ],
        ),
        out_shape=jax.ShapeDtypeStruct((db, N_HEADS * ds, HEAD_DIM), F32),
        compiler_params=_cparams(("arbitrary",)),
        name="dsa_sample",
    )(page_table, *ins, cache_ik, cache_k, cache_v)
    return out.reshape(db, N_HEADS, ds, HEAD_DIM).transpose(0, 2, 1, 3).reshape(db, ds, N_HEADS * HEAD_DIM)


def kernel(x_prompt, x_sample, cache_k, cache_v, cache_ik, state_pool, page_table, norm1_g, w_in,
           q_norm_g, k_norm_g, w_pool, pool_scale, w_o, norm2_g, w_pq, sub_keys, expert_u, expert_v):
    b, s, d = x_prompt.shape
    db, ds, _ = x_sample.shape
    past = page_table.shape[1] * PAGE_SIZE
    w_p = _pack_w_in(w_in[0])
    row = lambda a: a[0][None, :]
    shared = (row(norm1_g), w_p, row(q_norm_g), row(k_norm_g), w_pool[0].astype(BF16), row(pool_scale))

    (ya, q, k1, v1, kb, vb, iq, ik1, ikb, iw, pool_p) = _inproj_prompt(
        x_prompt, _rope_tables(jnp.arange(s)), *shared)
    yb = _dsa_prompt_pallas(q, kb, vb, iq, ikb, iw)

    x_tm = x_sample.transpose(1, 0, 2).reshape(ds * db, d)
    tab_s = _rope_tables(past + jnp.repeat(jnp.arange(ds), db))
    (ya_s, q_s, k2, v2, kb_s, vb_s, iq_s, ik2, ikb_s, iw_s, pool_s) = _inproj_sample(
        x_tm, tab_s, *shared, state_pool[0].transpose(1, 0, 2), db, ds, past)
    bm = lambda a: a.reshape(ds, db, -1).transpose(1, 0, 2)
    k2, v2, ik2 = bm(k2), bm(v2), bm(ik2)
    yb_s = _dsa_sample_pallas(bm(q_s), bm(kb_s), bm(vb_s), bm(iq_s), bm(ikb_s), bm(iw_s),
                              cache_k[0], cache_v[0], cache_ik[0], page_table).astype(BF16)

    n_p, n_s = b * s, db * ds
    t_pad = -(-(n_p + n_s) // PEER_TM) * PEER_TM
    cat = lambda p, q_: jnp.pad(jnp.concatenate([p.reshape(n_p, -1), q_.reshape(n_s, -1)], axis=0),
                                ((0, t_pad - n_p - n_s), (0, 0)))
    x2, h2t = _outproj(cat(x_prompt, x_sample), cat(ya, bm(ya_s)), cat(yb, yb_s), w_o[0], row(norm2_g))
    y = _peer(x2, h2t, w_pq[0], sub_keys[0], expert_u[0], expert_v[0])
    st = lambda a: a[None]
    return (y[:n_p].reshape(b, s, d), y[n_p:n_p + n_s].reshape(db, ds, d),
            st(k1), st(v1), st(ik1), st(pool_p[:, 1:]),
            st(k2), st(v2), st(ik2), st(pool_s.transpose(1, 0, 2)))
```

```python
import functools
import math

import jax
import jax.numpy as jnp
import numpy as np
from jax import lax
from jax.experimental import pallas as pl
from jax.experimental.pallas import tpu as pltpu

F32 = jnp.float32
BF16 = jnp.bfloat16
I32 = jnp.int32

LANES = 128
SUBLANES = 8
VMEM_LIMIT_BYTES = 56 * 1024 * 1024

D_MODEL = 1024
POOL_W = 512
POOL_WINDOWS = (2, 4, 8, 16)
POOL_GW = 128
POOL_STATE = 15
HEAD_DIM = 128
N_HEADS = 4
IDX_HEADS = 8
IDX_DIM = 64
TOPK_MAX = 256
ROPE_THETA = 10000.0
PAGE_SIZE = 128
N_KEYS = 128
PEER_HEADS = 8
PEER_TOPK = 16
PEER_HALF = 128
EPS = 1e-6
NEG_INF = float("-inf")


def _cparams(sem):
    return pltpu.CompilerParams(dimension_semantics=sem, vmem_limit_bytes=VMEM_LIMIT_BYTES)


PEER_TT = 128
CAND_ROWS = 16 + 8 * 15


def _top16_rows(x, iota0):
    vals, idxs = [], []
    for _ in range(PEER_TOPK):
        m = jnp.max(x, axis=0, keepdims=True)
        idx = jnp.min(jnp.where(x == m, iota0, N_KEYS), axis=0, keepdims=True)
        x = jnp.where(iota0 == idx, NEG_INF, x)
        vals.append(m)
        idxs.append(idx)
    return vals, idxs


def _peer_select_kernel(h2t_ref, wpqt_ref, sk_ref, n1_ref, w1_ref, r2_ref, p2_ref, s_scr):
    tt = h2t_ref.shape[1]
    qt = jnp.dot(wpqt_ref[...], h2t_ref[...], preferred_element_type=F32).astype(BF16)
    for hc in range(2 * PEER_HEADS):
        s_scr[hc] = jnp.dot(sk_ref[hc], qt[hc * PEER_HALF:(hc + 1) * PEER_HALF, :],
                            preferred_element_type=F32)

    iota0 = lax.broadcasted_iota(I32, (N_KEYS, tt), 0)
    rows = lax.broadcasted_iota(I32, (CAND_ROWS, tt), 0)
    ca = jnp.where(rows < 16, 0, ((rows - 16) >> 3) + 1)
    cb = jnp.where(rows < 16, rows, (rows - 16) & 7)
    cvalid = (ca + 1) * (cb + 1) <= PEER_TOPK
    cflat = ca * PEER_TOPK + cb

    n_cand = sum(1 for a in range(16) for b in range(16) if (a + 1) * (b + 1) <= PEER_TOPK)

    def head_fast(h):
        x1o = s_scr[2 * h]
        x2o = s_scr[2 * h + 1]
        sv1, sv2 = [], []
        x1 = x1o
        for _ in range(PEER_TOPK):
            m = jnp.max(x1, axis=0, keepdims=True)
            x1 = jnp.where(x1 == m, NEG_INF, x1)
            sv1.append(m)
        x2 = x2o
        r2 = jnp.full((N_KEYS, tt), float(N_KEYS - 1), F32)
        for r in range(PEER_TOPK):
            m = jnp.max(x2, axis=0, keepdims=True)
            hit = x2 == m
            x2 = jnp.where(hit, NEG_INF, x2)
            r2 = jnp.where(hit, float(r), r2)
            sv2.append(m)
        gone = lambda x: jnp.sum(jnp.where(x == NEG_INF, 1.0, 0.0), axis=0, keepdims=True)
        bad = (gone(x1) != PEER_TOPK) | (gone(x2) != PEER_TOPK)
        sv2a = jnp.concatenate(sv2, axis=0)
        cand = jnp.concatenate([sv1[0] + sv2a] + [sv1[a] + sv2a[:8] for a in range(1, 16)], axis=0)
        cand = jnp.where(cvalid, cand, NEG_INF)
        fv = []
        for _ in range(PEER_TOPK):
            m = jnp.max(cand, axis=0, keepdims=True)
            cand = jnp.where(cand == m, NEG_INF, cand)
            fv.append(m)
        bad = bad | (gone(cand) != PEER_TOPK + CAND_ROWS - n_cand)
        tau = fv[PEER_TOPK - 1]
        z = jnp.sum(jnp.exp(jnp.concatenate(fv, axis=0) - fv[0]), axis=0, keepdims=True)
        n1 = jnp.zeros((N_KEYS, tt), F32)
        for b in range(8):
            n1 = n1 + jnp.where(x1o + sv2[b] >= tau, 1.0, 0.0)
        n_top = jnp.sum(jnp.where(sv1[0] + sv2a >= tau, 1.0, 0.0), axis=0, keepdims=True)
        n1 = n1 + jnp.where(x1o == sv1[0], jnp.maximum(n_top - 8.0, 0.0), 0.0)
        bad = bad | (jnp.sum(n1, axis=0, keepdims=True) != PEER_TOPK)
        n1_ref[h] = n1
        w1_ref[h] = jnp.exp(x1o - sv1[0]) * (1.0 / z)
        r2_ref[h] = r2.astype(BF16)
        p2_ref[h] = jnp.exp(x2o - sv2[0]).astype(BF16)
        return bad

    def head(h, carry):
        bad = head_fast(h)

        @pl.when(jnp.max(jnp.where(bad, 1, 0)) > 0)
        def _():
            head_exact(h)

        return carry

    def head_exact(h):
        sv1, si1 = _top16_rows(s_scr[2 * h], iota0)
        x2 = s_scr[2 * h + 1]
        r2 = jnp.full((N_KEYS, tt), float(N_KEYS - 1), F32)
        p2 = jnp.zeros((N_KEYS, tt), F32)
        sv2 = []
        for r in range(PEER_TOPK):
            m = jnp.max(x2, axis=0, keepdims=True)
            idx = jnp.min(jnp.where(x2 == m, iota0, N_KEYS), axis=0, keepdims=True)
            hit = iota0 == idx
            x2 = jnp.where(hit, NEG_INF, x2)
            sv2.append(m)
            r2 = jnp.where(hit, float(r), r2)
            p2 = jnp.where(hit, jnp.exp(m - sv2[0]), p2)
        e1 = [jnp.exp(v - sv1[0]) for v in sv1]
        sv2a = jnp.concatenate(sv2, axis=0)
        e2a = jnp.exp(sv2a - sv2[0])
        cand = jnp.concatenate([sv1[0] + sv2a] + [sv1[a] + sv2a[:8] for a in range(1, 16)], axis=0)
        egrid = jnp.concatenate([e1[0] * e2a] + [e1[a] * e2a[:8] for a in range(1, 16)], axis=0)
        cand = jnp.where(cvalid, cand, NEG_INF)
        sel = jnp.zeros((CAND_ROWS, tt), F32)
        for _ in range(PEER_TOPK):
            m = jnp.max(cand, axis=0, keepdims=True)
            f = jnp.min(jnp.where(cand == m, cflat, PEER_TOPK * PEER_TOPK), axis=0, keepdims=True)
            hit = cflat == f
            cand = jnp.where(hit, NEG_INF, cand)
            sel = jnp.where(hit, 1.0, sel)
        z = jnp.sum(sel * egrid, axis=0, keepdims=True)
        inv_z = 1.0 / z
        n1 = jnp.zeros((N_KEYS, tt), F32)
        w1 = jnp.zeros((N_KEYS, tt), F32)
        for a in range(PEER_TOPK):
            lo = 0 if a == 0 else 16 + 8 * (a - 1)
            hi = 16 if a == 0 else lo + 8
            n_a = jnp.sum(sel[lo:hi], axis=0, keepdims=True)
            hit = iota0 == si1[a]
            n1 = jnp.where(hit, n_a, n1)
            w1 = jnp.where(hit, e1[a] * inv_z, w1)
        n1_ref[h] = n1
        w1_ref[h] = w1
        r2_ref[h] = r2.astype(BF16)
        p2_ref[h] = p2.astype(BF16)

    lax.fori_loop(0, PEER_HEADS, head, 0)


def _peer_select(h2t, wpqt, sk):
    t = h2t.shape[1]
    tt = PEER_TT
    assert t % tt == 0
    tok = lambda i: (0, 0, i)
    shp = (PEER_HEADS, N_KEYS, t)
    return pl.pallas_call(
        _peer_select_kernel,
        grid=(t // tt,),
        in_specs=[
            pl.BlockSpec((D_MODEL, tt), lambda i: (0, i)),
            pl.BlockSpec(wpqt.shape, lambda i: (0, 0)),
            pl.BlockSpec(sk.shape, lambda i: (0, 0, 0)),
        ],
        out_specs=[pl.BlockSpec((PEER_HEADS, N_KEYS, tt), tok)] * 4,
        out_shape=[jax.ShapeDtypeStruct(shp, F32), jax.ShapeDtypeStruct(shp, F32),
                   jax.ShapeDtypeStruct(shp, BF16), jax.ShapeDtypeStruct(shp, BF16)],
        scratch_shapes=[pltpu.VMEM((2 * PEER_HEADS, N_KEYS, tt), F32)],
        compiler_params=_cparams(("arbitrary",)),
        name="peer_select",
    )(h2t, wpqt, sk)


PEER_TM = 512
PEER_TE = 1024


def _gelu_exact(x):
    return 0.5 * x * (1.0 + lax.erf(x * (1.0 / math.sqrt(2.0))))


def _peer_expert_kernel(xt_ref, u_ref, vt_ref, n1_ref, w1_ref, r2_ref, p2_ref, x2_ref, out_ref,
                        acc_ref, sa_ref, sb_ref):
    j = pl.program_id(1)
    nc = PEER_TE // N_KEYS
    tm = xt_ref.shape[1]

    @pl.when(j == 0)
    def _():
        acc_ref[...] = jnp.zeros_like(acc_ref)
        sa_ref[...] = jnp.zeros_like(sa_ref)

    def step(s_prev_ref, s_next_ref):
        s_next_ref[...] = jnp.dot(u_ref[...], xt_ref[...], preferred_element_type=F32)
        c0 = pl.multiple_of(jnp.maximum(j - 1, 0) * nc, nc)
        n_rows = [n1_ref[h, pl.ds(c0, nc), :].astype(BF16) for h in range(PEER_HEADS)]
        w_rows = [w1_ref[h, pl.ds(c0, nc), :].astype(BF16) for h in range(PEER_HEADS)]
        parts = []
        for cc in range(nc):
            a = _gelu_exact(s_prev_ref[cc * N_KEYS:(cc + 1) * N_KEYS, :])
            g = jnp.zeros((N_KEYS, tm), BF16)
            for h in range(PEER_HEADS):
                nb = jnp.broadcast_to(n_rows[h][cc:cc + 1, :], (N_KEYS, tm))
                wb = jnp.broadcast_to(w_rows[h][cc:cc + 1, :], (N_KEYS, tm))
                g = g + jnp.where(r2_ref[h] < nb, p2_ref[h], jnp.zeros((), BF16)) * wb
            parts.append(a.astype(BF16) * g)
        acc_ref[...] += jnp.dot(vt_ref[...], jnp.concatenate(parts, axis=0), preferred_element_type=F32)

    @pl.when(j % 2 == 0)
    def _():
        step(sa_ref, sb_ref)

    @pl.when(j % 2 == 1)
    def _():
        step(sb_ref, sa_ref)

    @pl.when(j == pl.num_programs(1) - 1)
    def _():
        out_ref[...] = x2_ref[...] + acc_ref[...].T


def _peer_experts(xt, u, vt, n1, w1, r2, p2, x2):
    d, t = xt.shape
    e = u.shape[0]
    tm, te = PEER_TM, PEER_TE
    assert t % tm == 0 and e % te == 0
    nblk = e // te
    fac = pl.BlockSpec((PEER_HEADS, N_KEYS, tm), lambda i, j: (0, 0, i))
    return pl.pallas_call(
        _peer_expert_kernel,
        grid=(t // tm, nblk + 1),
        in_specs=[
            pl.BlockSpec((d, tm), lambda i, j: (0, i)),
            pl.BlockSpec((te, d), lambda i, j: (jnp.minimum(j, nblk - 1), 0)),
            pl.BlockSpec((d, te), lambda i, j: (0, jnp.maximum(j - 1, 0))),
            fac, fac, fac, fac,
            pl.BlockSpec((tm, d), lambda i, j: (i, 0)),
        ],
        out_specs=pl.BlockSpec((tm, d), lambda i, j: (i, 0)),
        out_shape=jax.ShapeDtypeStruct((t, d), F32),
        scratch_shapes=[pltpu.VMEM((d, tm), F32), pltpu.VMEM((te, tm), F32), pltpu.VMEM((te, tm), F32)],
        compiler_params=_cparams(("arbitrary", "arbitrary")),
        name="peer_experts",
    )(xt, u, vt, n1, w1, r2, p2, x2)


def _peer(x2, h2t, w_pq, sub_keys, expert_u, expert_v):
    wpqt = w_pq.T.astype(BF16)
    sk = sub_keys.reshape(2 * PEER_HEADS, N_KEYS, PEER_HALF).astype(BF16)
    n1, w1, r2, p2 = _peer_select(h2t, wpqt, sk)
    return _peer_experts(h2t, expert_u.astype(BF16), expert_v.T.astype(BF16), n1, w1, r2, p2, x2)


IN_TM = 256
C_U, C_Q, C_K, C_V, C_IQ, C_IK, C_IW, C_END = 0, 512, 1024, 1152, 1280, 1792, 1920, 2048
POOL_HALO = 16


def _pack_w_in(w_in):
    o = np.cumsum((0, POOL_W, N_HEADS * HEAD_DIM, HEAD_DIM, HEAD_DIM, IDX_HEADS * IDX_DIM, IDX_DIM, IDX_HEADS))
    z = lambda n: jnp.zeros((D_MODEL, n), w_in.dtype)
    return jnp.concatenate([w_in[:, o[0]:o[5]], w_in[:, o[5]:o[6]], z(C_IW - C_IK - IDX_DIM),
                            w_in[:, o[6]:o[7]], z(C_END - C_IW - IDX_HEADS)], axis=1).astype(BF16)


def _rope_tables(pos):
    def tab(half, reps):
        inv = ROPE_THETA ** (-jnp.arange(half, dtype=F32) / half)
        ang = pos.astype(F32)[:, None] * inv[None, :]
        c, s = jnp.cos(ang), jnp.sin(ang)
        return jnp.tile(jnp.concatenate([c, c], 1), (1, reps)), jnp.tile(jnp.concatenate([-s, s], 1), (1, reps))
    c128, s128 = tab(HEAD_DIM // 2, 1)
    c64, s64 = tab(IDX_DIM // 2, 2)
    return jnp.concatenate([c128, s128, c64, s64], axis=1)


def _rope128(x, tab):
    return x * tab[:, 0:128] + pltpu.roll(x, 64, 1) * tab[:, 128:256]


def _rope64x2(x, tab):
    lane = lax.broadcasted_iota(I32, x.shape, 1)
    partner = jnp.where((lane & 63) < 32, pltpu.roll(x, 96, 1), pltpu.roll(x, 32, 1))
    return x * tab[:, 256:384] + partner * tab[:, 384:512]


def _head_norm(x, g):
    return x * lax.rsqrt(jnp.mean(x * x, axis=-1, keepdims=True) + EPS) * g


def _project_block(x_ref, tab_ref, g1_ref, w_ref, qg_ref, kg_ref,
                   q_ref, k_ref, v_ref, kb_ref, vb_ref, iq_ref, ik_ref, ikb_ref, iw_ref, keys_major=False):
    x = x_ref[...]
    h = x * lax.rsqrt(jnp.mean(x * x, axis=-1, keepdims=True) + EPS) * g1_ref[...]
    z = jnp.dot(h.astype(BF16), w_ref[...], preferred_element_type=F32)
    tab = tab_ref[...]
    for hh in range(N_HEADS):
        sl = slice(C_Q + hh * HEAD_DIM, C_Q + (hh + 1) * HEAD_DIM)
        q_ref[:, hh * HEAD_DIM:(hh + 1) * HEAD_DIM] = _rope128(_head_norm(z[:, sl], qg_ref[...]), tab).astype(BF16)
    k = _rope128(_head_norm(z[:, C_K:C_V], kg_ref[...]), tab)
    v = z[:, C_V:C_IQ]
    k_ref[...] = k
    v_ref[...] = v
    kb_ref[...] = k.astype(BF16)
    vb_ref[...] = (v.T if keys_major else v).astype(BF16)
    for hp in range(IDX_HEADS // 2):
        sl = slice(C_IQ + hp * 128, C_IQ + (hp + 1) * 128)
        iq_ref[:, hp * 128:(hp + 1) * 128] = _rope64x2(z[:, sl], tab).astype(BF16)
    ik = _rope64x2(z[:, C_IK:C_IW], tab)[:, :IDX_DIM]
    ik_ref[...] = ik
    ikb_ref[...] = ik.astype(BF16)
    iw = z[:, C_IW:C_END] * (IDX_HEADS ** -0.5)
    iw_ref[...] = iw.T[:IDX_HEADS, :] if keys_major else iw[:, :IDX_HEADS]
    return z[:, C_U:C_Q]


def _pool_out(d_list, wp_ref, ps_ref, ya_ref, rows):
    for g in range(len(POOL_WINDOWS)):
        y = jnp.dot(d_list[g].astype(BF16), wp_ref[g], preferred_element_type=F32)
        ya_ref[rows, g * POOL_GW:(g + 1) * POOL_GW] = (y * ps_ref[:, g * POOL_GW:(g + 1) * POOL_GW]).astype(BF16)


def _inproj_prompt_kernel(x_ref, tab_ref, g1_ref, w_ref, qg_ref, kg_ref, wp_ref, ps_ref,
                          ya_ref, q_ref, k_ref, v_ref, kb_ref, vb_ref, iq_ref, ik_ref, ikb_ref, iw_ref,
                          pool_ref, ext_scr):
    tm = x_ref.shape[0]
    blk = pl.program_id(1)

    @pl.when(blk == 0)
    def _():
        ext_scr[0:POOL_HALO, :] = jnp.zeros((POOL_HALO, POOL_W), F32)

    u = _project_block(x_ref, tab_ref, g1_ref, w_ref, qg_ref, kg_ref,
                       q_ref, k_ref, v_ref, kb_ref, vb_ref, iq_ref, ik_ref, ikb_ref, iw_ref, keys_major=True)
    ext_scr[POOL_HALO:POOL_HALO + tm, :] = u
    pos = blk * tm + lax.broadcasted_iota(I32, (tm, 1), 0)
    d_list = []
    for g, w in enumerate(POOL_WINDOWS):
        sl = slice(g * POOL_GW, (g + 1) * POOL_GW)
        acc = u[:, sl]
        for j in range(1, w):
            acc = acc + ext_scr[POOL_HALO - j:POOL_HALO - j + tm, sl]
        cnt = jnp.minimum(pos + 1, w).astype(F32)
        d_list.append(acc / cnt - u[:, sl])
    _pool_out(d_list, wp_ref, ps_ref, ya_ref, slice(None))
    tail = ext_scr[tm:tm + POOL_HALO, :]
    pool_ref[...] = tail
    ext_scr[0:POOL_HALO, :] = tail


def _inproj_outs(n, lead):
    sd = lambda w, dt: jax.ShapeDtypeStruct(lead + (n, w), dt)
    return [sd(POOL_W, BF16), sd(N_HEADS * HEAD_DIM, BF16), sd(HEAD_DIM, F32), sd(HEAD_DIM, F32),
            sd(HEAD_DIM, BF16), sd(HEAD_DIM, BF16), sd(IDX_HEADS * IDX_DIM, BF16), sd(IDX_DIM, F32),
            sd(IDX_DIM, BF16), sd(IDX_HEADS, F32)]


def _inproj_prompt(x, tab, g1, w_p, qg, kg, wp, ps):
    b, s, d = x.shape
    tm = IN_TM
    assert s % tm == 0
    full = lambda a: pl.BlockSpec(a.shape, lambda bi, i: (0,) * a.ndim)
    rows = lambda w: pl.BlockSpec((None, tm, w), lambda bi, i: (bi, i, 0))
    cols = lambda h: pl.BlockSpec((None, h, tm), lambda bi, i: (bi, 0, i))
    outs = _inproj_outs(s, (b,)) + [jax.ShapeDtypeStruct((b, POOL_HALO, POOL_W), F32)]
    out_specs = [rows(o.shape[-1]) for o in outs[:-1]] + [
        pl.BlockSpec((None, POOL_HALO, POOL_W), lambda bi, i: (bi, 0, 0))]
    outs[5] = jax.ShapeDtypeStruct((b, HEAD_DIM, s), BF16)
    outs[9] = jax.ShapeDtypeStruct((b, IDX_HEADS, s), F32)
    out_specs[5], out_specs[9] = cols(HEAD_DIM), cols(IDX_HEADS)
    return pl.pallas_call(
        _inproj_prompt_kernel,
        grid=(b, s // tm),
        in_specs=[rows(d), pl.BlockSpec((tm, tab.shape[1]), lambda bi, i: (i, 0)),
                  full(g1), full(w_p), full(qg), full(kg), full(wp), full(ps)],
        out_specs=out_specs,
        out_shape=outs,
        scratch_shapes=[pltpu.VMEM((POOL_HALO + tm, POOL_W), F32)],
        compiler_params=_cparams(("arbitrary", "arbitrary")),
        name="inproj_prompt",
    )(x, tab, g1, w_p, qg, kg, wp, ps)


def _inproj_sample_kernel(x_ref, tab_ref, g1_ref, w_ref, qg_ref, kg_ref, wp_ref, ps_ref, st_ref,
                          ya_ref, q_ref, k_ref, v_ref, kb_ref, vb_ref, iq_ref, ik_ref, ikb_ref, iw_ref,
                          pool_ref, *, n_seq, n_new, past):
    u = _project_block(x_ref, tab_ref, g1_ref, w_ref, qg_ref, kg_ref,
                       q_ref, k_ref, v_ref, kb_ref, vb_ref, iq_ref, ik_ref, ikb_ref, iw_ref)
    hist = [st_ref[i] for i in range(POOL_STATE)] + [u[j * n_seq:(j + 1) * n_seq, :] for j in range(n_new)]
    for j in range(n_new):
        d_list = []
        for g, w in enumerate(POOL_WINDOWS):
            sl = slice(g * POOL_GW, (g + 1) * POOL_GW)
            acc = hist[POOL_STATE + j][:, sl]
            for i in range(1, w):
                acc = acc + hist[POOL_STATE + j - i][:, sl]
            d_list.append(acc / float(min(past + j + 1, w)) - hist[POOL_STATE + j][:, sl])
        _pool_out(d_list, wp_ref, ps_ref, ya_ref, slice(j * n_seq, (j + 1) * n_seq))
    for i in range(POOL_STATE):
        pool_ref[i] = hist[n_new + i]


def _inproj_sample(x_tm, tab, g1, w_p, qg, kg, wp, ps, st_tm, n_seq, n_new, past):
    n = x_tm.shape[0]
    outs = _inproj_outs(n, ()) + [jax.ShapeDtypeStruct((POOL_STATE, n_seq, POOL_W), F32)]
    return pl.pallas_call(
        functools.partial(_inproj_sample_kernel, n_seq=n_seq, n_new=n_new, past=past),
        out_shape=outs,
        compiler_params=pltpu.CompilerParams(vmem_limit_bytes=VMEM_LIMIT_BYTES),
        name="inproj_sample",
    )(x_tm, tab, g1, w_p, qg, kg, wp, ps, st_tm)


OUT_TM = 512


def _outproj_kernel(x_ref, ya_ref, yb_ref, wa_ref, wb_ref, g2_ref, x2_ref, h2t_ref):
    x2 = (x_ref[...] + jnp.dot(ya_ref[...], wa_ref[...], preferred_element_type=F32)
          + jnp.dot(yb_ref[...], wb_ref[...], preferred_element_type=F32))
    x2_ref[...] = x2
    h2 = x2 * lax.rsqrt(jnp.mean(x2 * x2, axis=-1, keepdims=True) + EPS) * g2_ref[...]
    h2t_ref[...] = h2.T.astype(BF16)


def _outproj(x, ya, yb, w_o, g2):
    t, d = x.shape
    tm = OUT_TM
    assert t % tm == 0
    wa = w_o[:POOL_W].astype(BF16)
    wb = w_o[POOL_W:].astype(BF16)
    rows = lambda w: pl.BlockSpec((tm, w), lambda i: (i, 0))
    full = lambda a: pl.BlockSpec(a.shape, lambda i: (0,) * a.ndim)
    return pl.pallas_call(
        _outproj_kernel,
        grid=(t // tm,),
        in_specs=[rows(d), rows(POOL_W), rows(yb.shape[1]), full(wa), full(wb), full(g2)],
        out_specs=[rows(d), pl.BlockSpec((d, tm), lambda i: (0, i))],
        out_shape=[jax.ShapeDtypeStruct((t, d), F32), jax.ShapeDtypeStruct((d, t), BF16)],
        compiler_params=_cparams(("arbitrary",)),
        name="outproj",
    )(x, ya, yb, wa, wb, g2)


DSA_TQ = 256
INT_MIN = -2 ** 31


def _order_key(score):
    b = pltpu.bitcast(score + 0.0, I32)
    return jnp.where(b < 0, b ^ 0x7FFFFFFF, b)


def _kth_largest_key(count_ge, k, shape):
    def bit_step(i, t_u):
        cand_u = t_u | (jnp.int32(1) << (31 - i))
        cnt = count_ge(cand_u ^ INT_MIN)
        return jnp.where(cnt >= k, cand_u, t_u)
    t_u = lax.fori_loop(0, 32, bit_step, jnp.zeros(shape, I32))
    return t_u ^ INT_MIN


def _dsa_prompt_kernel(iq_ref, iwt_ref, q_ref, ik_ref, k_ref, vt_ref, o_ref,
                       key_scr, tie_scr, m_scr, l_scr, acc_scr, carry_scr, *, k_sel):
    tq = DSA_TQ
    qb = pl.program_id(1)
    n_chunks = qb + 1
    krow = lax.broadcasted_iota(I32, (tq, tq), 0)
    qcol = lax.broadcasted_iota(I32, (tq, tq), 1)
    nt = (((1,), (1,)), ((), ()))

    def chunk_start(j):
        return pl.multiple_of(j * tq, tq)

    def score_chunk(j, c):
        ikc = ik_ref[0, pl.ds(chunk_start(j), tq), :]
        sc = jnp.zeros((tq, tq), F32)
        for h in range(IDX_HEADS):
            s = lax.dot_general(ikc, iq_ref[0, :, h * IDX_DIM:(h + 1) * IDX_DIM], nt,
                                preferred_element_type=F32)
            sc = sc + jnp.maximum(s, 0.0) * iwt_ref[0, h:h + 1, :]
        key = _order_key(sc)
        key = jnp.where((j < qb) | (krow <= qcol), key, INT_MIN)
        key_scr[pl.ds(chunk_start(j), tq), :] = key
        return c

    lax.fori_loop(0, n_chunks, score_chunk, 0)

    def count(pred):
        def body(j, acc):
            kc = key_scr[pl.ds(chunk_start(j), tq), :]
            m = jnp.where(pred(kc), 1, 0)
            return acc + jnp.sum(m.reshape(tq // SUBLANES, SUBLANES, tq), axis=0)
        acc = lax.fori_loop(0, n_chunks, body, jnp.zeros((SUBLANES, tq), I32))
        return jnp.sum(acc, axis=0, keepdims=True)

    thr = _kth_largest_key(lambda t: count(lambda kc: kc >= t), k_sel, (1, tq))
    n_gt = count(lambda kc: kc > thr)
    n_eq = count(lambda kc: (kc == thr) & (kc > INT_MIN))
    room = k_sel - n_gt
    ties_cut = jnp.max(n_eq - room) > 0

    m_scr[...] = jnp.full(m_scr.shape, NEG_INF, F32)
    l_scr[...] = jnp.zeros(l_scr.shape, F32)
    acc_scr[...] = jnp.zeros(acc_scr.shape, F32)
    carry_scr[...] = jnp.zeros(carry_scr.shape, F32)

    def attend_chunk(j, c):
        kc = key_scr[pl.ds(chunk_start(j), tq), :]
        eq = (kc == thr) & (kc > INT_MIN)

        @pl.when(ties_cut)
        def _():
            tri = jnp.where(qcol <= krow, 1.0, 0.0).astype(BF16)
            eqf = jnp.where(eq, 1.0, 0.0)
            prefix = jnp.dot(tri, eqf.astype(BF16), preferred_element_type=F32) + carry_scr[...]
            tie_scr[...] = jnp.where(prefix <= room.astype(F32), 1.0, 0.0)
            carry_scr[...] = carry_scr[...] + jnp.sum(eqf, axis=0, keepdims=True)

        @pl.when(jnp.logical_not(ties_cut))
        def _():
            tie_scr[...] = jnp.ones(tie_scr.shape, F32)

        sel = (kc > thr) | (eq & (tie_scr[...] > 0.0))
        kch = k_ref[0, pl.ds(chunk_start(j), tq), :]
        vtc = vt_ref[0, :, pl.ds(chunk_start(j), tq)]
        for h in range(N_HEADS):
            qh = q_ref[0, :, h * HEAD_DIM:(h + 1) * HEAD_DIM]
            lg = lax.dot_general(kch, qh, nt, preferred_element_type=F32) * (HEAD_DIM ** -0.5)
            lg = jnp.where(sel, lg, NEG_INF)
            m_old = m_scr[h]
            m_new = jnp.maximum(m_old, jnp.max(lg, axis=0, keepdims=True))
            m_safe = jnp.where(m_new == NEG_INF, 0.0, m_new)
            alpha = jnp.exp(m_old - m_safe)
            p = jnp.exp(lg - m_safe)
            l_scr[h] = alpha * l_scr[h] + jnp.sum(p, axis=0, keepdims=True)
            acc_scr[h] = alpha * acc_scr[h] + jnp.dot(vtc, p.astype(BF16), preferred_element_type=F32)
            m_scr[h] = m_new
        return c

    lax.fori_loop(0, n_chunks, attend_chunk, 0)
    for h in range(N_HEADS):
        o_ref[0, :, h * HEAD_DIM:(h + 1) * HEAD_DIM] = (acc_scr[h] / l_scr[h]).T.astype(o_ref.dtype)


def _dsa_prompt_pallas(q, k, vt, iq, ik, iwt):
    b, s, _ = q.shape
    tq = DSA_TQ
    assert s % tq == 0
    k_sel = min(TOPK_MAX, s // 4)
    blk = lambda w: pl.BlockSpec((1, tq, w), lambda bi, qi: (bi, qi, 0))
    seq = lambda w: pl.BlockSpec((1, s, w), lambda bi, qi: (bi, 0, 0))
    return pl.pallas_call(
        functools.partial(_dsa_prompt_kernel, k_sel=k_sel),
        grid=(b, s // tq),
        in_specs=[blk(IDX_HEADS * IDX_DIM),
                  pl.BlockSpec((1, IDX_HEADS, tq), lambda bi, qi: (bi, 0, qi)),
                  blk(N_HEADS * HEAD_DIM), seq(IDX_DIM), seq(HEAD_DIM),
                  pl.BlockSpec((1, HEAD_DIM, s), lambda bi, qi: (bi, 0, 0))],
        out_specs=blk(N_HEADS * HEAD_DIM),
        out_shape=jax.ShapeDtypeStruct((b, s, N_HEADS * HEAD_DIM), BF16),
        scratch_shapes=[
            pltpu.VMEM((s, tq), I32),
            pltpu.VMEM((tq, tq), F32),
            pltpu.VMEM((N_HEADS, 1, tq), F32),
            pltpu.VMEM((N_HEADS, 1, tq), F32),
            pltpu.VMEM((N_HEADS, HEAD_DIM, tq), F32),
            pltpu.VMEM((1, tq), F32),
        ],
        compiler_params=_cparams(("arbitrary", "arbitrary")),
        name="dsa_prompt",
    )(iq, iwt, q, ik, k, vt)


DEC_ROWS = 8
NEW_PAD = 128


def _dsa_sample_kernel(pt_ref, iq_ref, wsel_ref, q_ref, ikn_ref, kn_ref, vn_ref, cik_ref, ck_ref, cv_ref,
                       o_ref, ikbuf, kbuf, vbuf, sem, tie_scr, *, n_pages, n_new, k_sel):
    bi = pl.program_id(0)
    nb = pl.num_programs(0)
    slot = bi % 2
    n_past = n_pages * PAGE_SIZE
    nt = (((1,), (1,)), ((), ()))

    def page_copies(seq, sl):
        cps = []
        for p in range(n_pages):
            pg = pt_ref[seq, p]
            cps.append(pltpu.make_async_copy(cik_ref.at[pg], ikbuf.at[sl, p], sem.at[sl, 0]))
            cps.append(pltpu.make_async_copy(ck_ref.at[pg], kbuf.at[sl, p], sem.at[sl, 1]))
            cps.append(pltpu.make_async_copy(cv_ref.at[pg], vbuf.at[sl, p], sem.at[sl, 2]))
        return cps

    @pl.when(bi == 0)
    def _():
        for cp in page_copies(0, 0):
            cp.start()

    @pl.when(bi + 1 < nb)
    def _():
        for cp in page_copies(bi + 1, 1 - slot):
            cp.start()

    for cp in page_copies(bi, slot):
        cp.wait()

    iq = iq_ref[...]
    wsel = wsel_ref[...]
    ik_all = ikbuf[slot].reshape(n_past, IDX_DIM).astype(BF16)
    s_past = lax.dot_general(iq, ik_all, nt, preferred_element_type=F32)
    s_new = lax.dot_general(iq, ikn_ref[...], nt, preferred_element_type=F32)
    hi = lax.Precision.HIGHEST
    sc_past = jnp.dot(wsel, jnp.maximum(s_past, 0.0), precision=hi, preferred_element_type=F32)
    sc_new = jnp.dot(wsel, jnp.maximum(s_new, 0.0), precision=hi, preferred_element_type=F32)
    row_p = lax.broadcasted_iota(I32, (DEC_ROWS, n_past), 0)
    row_n = lax.broadcasted_iota(I32, (DEC_ROWS, NEW_PAD), 0)
    col_n = lax.broadcasted_iota(I32, (DEC_ROWS, NEW_PAD), 1)
    key_p = jnp.where(row_p < n_new, _order_key(sc_past), INT_MIN)
    key_n = jnp.where((row_n < n_new) & (col_n <= row_n), _order_key(sc_new), INT_MIN)

    def count(pred):
        one = lambda m: jnp.sum(jnp.where(m, 1, 0), axis=1, keepdims=True)
        return one(pred(key_p)) + one(pred(key_n))

    thr = _kth_largest_key(lambda t: count(lambda kc: kc >= t), k_sel, (DEC_ROWS, 1))
    n_gt = count(lambda kc: kc > thr)
    eq_p = (key_p == thr) & (key_p > INT_MIN)
    eq_n = (key_n == thr) & (key_n > INT_MIN)
    n_eq = (jnp.sum(jnp.where(eq_p, 1, 0), axis=1, keepdims=True)
            + jnp.sum(jnp.where(eq_n, 1, 0), axis=1, keepdims=True))
    room = k_sel - n_gt
    ties_cut = jnp.max(n_eq - room) > 0

    @pl.when(ties_cut)
    def _():
        r = lax.broadcasted_iota(I32, (LANES, LANES), 0)
        c = lax.broadcasted_iota(I32, (LANES, LANES), 1)
        tri = jnp.where(r <= c, 1.0, 0.0).astype(BF16)
        roomf = room.astype(F32)
        carry = jnp.zeros((DEC_ROWS, 1), F32)
        eqf_p = jnp.where(eq_p, 1.0, 0.0)
        for ch in range(n_past // LANES):
            e = eqf_p[:, ch * LANES:(ch + 1) * LANES]
            prefix = jnp.dot(e.astype(BF16), tri, preferred_element_type=F32) + carry
            tie_scr[:, ch * LANES:(ch + 1) * LANES] = jnp.where(prefix <= roomf, 1.0, 0.0)
            carry = carry + jnp.sum(e, axis=1, keepdims=True)
        e = jnp.where(eq_n, 1.0, 0.0)
        prefix = jnp.dot(e.astype(BF16), tri, preferred_element_type=F32) + carry
        tie_scr[:, n_past:n_past + NEW_PAD] = jnp.where(prefix <= roomf, 1.0, 0.0)

    @pl.when(jnp.logical_not(ties_cut))
    def _():
        tie_scr[...] = jnp.ones(tie_scr.shape, F32)

    sel_p = jnp.where((key_p > thr) | (eq_p & (tie_scr[:, 0:n_past] > 0.0)), 1.0, 0.0)
    sel_n = jnp.where((key_n > thr) | (eq_n & (tie_scr[:, n_past:n_past + NEW_PAD] > 0.0)), 1.0, 0.0)
    rr = lax.broadcasted_iota(I32, (N_HEADS * n_new, DEC_ROWS), 0)
    rc = lax.broadcasted_iota(I32, (N_HEADS * n_new, DEC_ROWS), 1)
    hit = rr == rc
    for hh in range(1, N_HEADS):
        hit = hit | (rr == rc + hh * n_new)
    rep = jnp.where(hit & (rc < n_new), 1.0, 0.0).astype(BF16)
    m_p = jnp.dot(rep, sel_p.astype(BF16), preferred_element_type=F32)
    m_n = jnp.dot(rep, sel_n.astype(BF16), preferred_element_type=F32)

    q = q_ref[...]
    scale = HEAD_DIM ** -0.5
    k_all = kbuf[slot].reshape(n_past, HEAD_DIM).astype(BF16)
    lg_p = jnp.where(m_p > 0.5, lax.dot_general(q, k_all, nt, preferred_element_type=F32) * scale, NEG_INF)
    lg_n = jnp.where(m_n > 0.5, lax.dot_general(q, kn_ref[...], nt, preferred_element_type=F32) * scale,
                     NEG_INF)
    mx = jnp.maximum(jnp.max(lg_p, axis=1, keepdims=True), jnp.max(lg_n, axis=1, keepdims=True))
    p_p = jnp.exp(lg_p - mx)
    p_n = jnp.exp(lg_n - mx)
    den = jnp.sum(p_p, axis=1, keepdims=True) + jnp.sum(p_n, axis=1, keepdims=True)
    v_all = vbuf[slot].reshape(n_past, HEAD_DIM).astype(BF16)
    num = (jnp.dot(p_p.astype(BF16), v_all, preferred_element_type=F32)
           + jnp.dot(p_n.astype(BF16), vn_ref[...], preferred_element_type=F32))
    o_ref[...] = num / den


def _dsa_sample_pallas(q, kb, vb, iq, ikb, iw, cache_k, cache_v, cache_ik, page_table):
    db, ds, _ = q.shape
    n_pages = page_table.shape[1]
    n_past = n_pages * PAGE_SIZE
    assert ds <= DEC_ROWS and ds <= NEW_PAD
    k_sel = min(TOPK_MAX, (n_past + ds) // 4)
    heads_first = lambda a, nh: a.reshape(db, ds, nh, -1).transpose(0, 2, 1, 3).reshape(db, nh * ds, -1)
    iq_r = heads_first(iq, IDX_HEADS)
    q_r = heads_first(q, N_HEADS)
    wsel = jnp.einsum('bjh,jk->bjhk', iw, jnp.eye(ds, dtype=F32)).reshape(db, ds, IDX_HEADS * ds)
    wsel = jnp.pad(wsel, ((0, 0), (0, DEC_ROWS - ds), (0, 0)))
    pad_new = lambda a: jnp.pad(a, ((0, 0), (0, NEW_PAD - ds), (0, 0)))
    per_seq = lambda a: pl.BlockSpec((None,) + a.shape[1:], lambda i, pt: (i, 0, 0))
    any_spec = pl.BlockSpec(memory_space=pl.ANY)
    ins = (iq_r, wsel, q_r, pad_new(ikb), pad_new(kb), pad_new(vb))
    out = pl.pallas_call(
        functools.partial(_dsa_sample_kernel, n_pages=n_pages, n_new=ds, k_sel=k_sel),
        grid_spec=pltpu.PrefetchScalarGridSpec(
            num_scalar_prefetch=1,
            grid=(db,),
            in_specs=[per_seq(a) for a in ins] + [any_spec, any_spec, any_spec],
            out_specs=pl.BlockSpec((None, N_HEADS * ds, HEAD_DIM), lambda i, pt: (i, 0, 0)),
            scratch_shapes=[
                pltpu.VMEM((2, n_pages, PAGE_SIZE, IDX_DIM), F32),
                pltpu.VMEM((2, n_pages, PAGE_SIZE, HEAD_DIM), F32),
                pltpu.VMEM((2, n_pages, PAGE_SIZE, HEAD_DIM), F32),
                pltpu.SemaphoreType.DMA((2, 3)),
                pltpu.VMEM((DEC_ROWS, n_past + NEW_PAD), F32),
            ],
        ),
        out_shape=jax.ShapeDtypeStruct((db, N_HEADS * ds, HEAD_DIM), F32),
        compiler_params=_cparams(("arbitrary",)),
        name="dsa_sample",
    )(page_table, *ins, cache_ik, cache_k, cache_v)
    return out.reshape(db, N_HEADS, ds, HEAD_DIM).transpose(0, 2, 1, 3).reshape(db, ds, N_HEADS * HEAD_DIM)


def kernel(x_prompt, x_sample, cache_k, cache_v, cache_ik, state_pool, page_table, norm1_g, w_in,
           q_norm_g, k_norm_g, w_pool, pool_scale, w_o, norm2_g, w_pq, sub_keys, expert_u, expert_v):
    b, s, d = x_prompt.shape
    db, ds, _ = x_sample.shape
    past = page_table.shape[1] * PAGE_SIZE
    w_p = _pack_w_in(w_in[0])
    row = lambda a: a[0][None, :]
    shared = (row(norm1_g), w_p, row(q_norm_g), row(k_norm_g), w_pool[0].astype(BF16), row(pool_scale))

    (ya, q, k1, v1, kb, vbt, iq, ik1, ikb, iwt, pool_p) = _inproj_prompt(
        x_prompt, _rope_tables(jnp.arange(s)), *shared)
    yb = _dsa_prompt_pallas(q, kb, vbt, iq, ikb, iwt)

    x_tm = x_sample.transpose(1, 0, 2).reshape(ds * db, d)
    tab_s = _rope_tables(past + jnp.repeat(jnp.arange(ds), db))
    (ya_s, q_s, k2, v2, kb_s, vb_s, iq_s, ik2, ikb_s, iw_s, pool_s) = _inproj_sample(
        x_tm, tab_s, *shared, state_pool[0].transpose(1, 0, 2), db, ds, past)
    bm = lambda a: a.reshape(ds, db, -1).transpose(1, 0, 2)
    k2, v2, ik2 = bm(k2), bm(v2), bm(ik2)
    yb_s = _dsa_sample_pallas(bm(q_s), bm(kb_s), bm(vb_s), bm(iq_s), bm(ikb_s), bm(iw_s),
                              cache_k[0], cache_v[0], cache_ik[0], page_table).astype(BF16)

    n_p, n_s = b * s, db * ds
    t_pad = -(-(n_p + n_s) // PEER_TM) * PEER_TM
    cat = lambda p, q_: jnp.pad(jnp.concatenate([p.reshape(n_p, -1), q_.reshape(n_s, -1)], axis=0),
                                ((0, t_pad - n_p - n_s), (0, 0)))
    x2, h2t = _outproj(cat(x_prompt, x_sample), cat(ya, bm(ya_s)), cat(yb, yb_s), w_o[0], row(norm2_g))
    y = _peer(x2, h2t, w_pq[0], sub_keys[0], expert_u[0], expert_v[0])
    st = lambda a: a[None]
    return (y[:n_p].reshape(b, s, d), y[n_p:n_p + n_s].reshape(db, ds, d),
            st(k1), st(v1), st(ik1), st(pool_p[:, 1:]),
            st(k2), st(v2), st(ik2), st(pool_s.transpose(1, 0, 2)))
```

```python
import functools
import math

import jax
import jax.numpy as jnp
import numpy as np
from jax import lax
from jax.experimental import pallas as pl
from jax.experimental.pallas import tpu as pltpu

F32 = jnp.float32
BF16 = jnp.bfloat16
I32 = jnp.int32

LANES = 128
SUBLANES = 8
VMEM_LIMIT_BYTES = 56 * 1024 * 1024

D_MODEL = 1024
POOL_W = 512
POOL_WINDOWS = (2, 4, 8, 16)
POOL_GW = 128
POOL_STATE = 15
HEAD_DIM = 128
N_HEADS = 4
IDX_HEADS = 8
IDX_DIM = 64
TOPK_MAX = 256
ROPE_THETA = 10000.0
PAGE_SIZE = 128
N_KEYS = 128
PEER_HEADS = 8
PEER_TOPK = 16
PEER_HALF = 128
EPS = 1e-6
NEG_INF = float("-inf")


def _cparams(sem):
    return pltpu.CompilerParams(dimension_semantics=sem, vmem_limit_bytes=VMEM_LIMIT_BYTES)


PEER_TT = 512
CAND_ROWS = 16 + 8 * 15


def _top16_rows(x, iota0):
    vals, idxs = [], []
    for _ in range(PEER_TOPK):
        m = jnp.max(x, axis=0, keepdims=True)
        idx = jnp.min(jnp.where(x == m, iota0, N_KEYS), axis=0, keepdims=True)
        x = jnp.where(iota0 == idx, NEG_INF, x)
        vals.append(m)
        idxs.append(idx)
    return vals, idxs


def _peer_select_kernel(h2t_ref, wpqt_ref, sk_ref, n1_ref, w1_ref, r2_ref, p2_ref, s_scr):
    tt = h2t_ref.shape[1]
    qt = jnp.dot(wpqt_ref[...], h2t_ref[...], preferred_element_type=F32).astype(BF16)
    for hc in range(2 * PEER_HEADS):
        s_scr[hc] = jnp.dot(sk_ref[hc], qt[hc * PEER_HALF:(hc + 1) * PEER_HALF, :],
                            preferred_element_type=F32)

    iota0 = lax.broadcasted_iota(I32, (N_KEYS, tt), 0)
    rows = lax.broadcasted_iota(I32, (CAND_ROWS, tt), 0)
    ca = jnp.where(rows < 16, 0, ((rows - 16) >> 3) + 1)
    cb = jnp.where(rows < 16, rows, (rows - 16) & 7)
    cvalid = (ca + 1) * (cb + 1) <= PEER_TOPK
    cflat = ca * PEER_TOPK + cb

    n_cand = sum(1 for a in range(16) for b in range(16) if (a + 1) * (b + 1) <= PEER_TOPK)

    def head_fast(h):
        x1o = s_scr[2 * h]
        x2o = s_scr[2 * h + 1]
        sv1, sv2 = [], []
        x1 = x1o
        for _ in range(PEER_TOPK):
            m = jnp.max(x1, axis=0, keepdims=True)
            x1 = jnp.where(x1 == m, NEG_INF, x1)
            sv1.append(m)
        x2 = x2o
        r2 = jnp.full((N_KEYS, tt), float(N_KEYS - 1), F32)
        for r in range(PEER_TOPK):
            m = jnp.max(x2, axis=0, keepdims=True)
            hit = x2 == m
            x2 = jnp.where(hit, NEG_INF, x2)
            r2 = jnp.where(hit, float(r), r2)
            sv2.append(m)
        gone = lambda x: jnp.sum(jnp.where(x == NEG_INF, 1.0, 0.0), axis=0, keepdims=True)
        bad = (gone(x1) != PEER_TOPK) | (gone(x2) != PEER_TOPK)
        sv2a = jnp.concatenate(sv2, axis=0)
        cand = jnp.concatenate([sv1[0] + sv2a] + [sv1[a] + sv2a[:8] for a in range(1, 16)], axis=0)
        cand = jnp.where(cvalid, cand, NEG_INF)
        fv = []
        for _ in range(PEER_TOPK):
            m = jnp.max(cand, axis=0, keepdims=True)
            cand = jnp.where(cand == m, NEG_INF, cand)
            fv.append(m)
        bad = bad | (gone(cand) != PEER_TOPK + CAND_ROWS - n_cand)
        tau = fv[PEER_TOPK - 1]
        z = jnp.sum(jnp.exp(jnp.concatenate(fv, axis=0) - fv[0]), axis=0, keepdims=True)
        n1 = jnp.zeros((N_KEYS, tt), F32)
        for b in range(8):
            n1 = n1 + jnp.where(x1o + sv2[b] >= tau, 1.0, 0.0)
        n_top = jnp.sum(jnp.where(sv1[0] + sv2a >= tau, 1.0, 0.0), axis=0, keepdims=True)
        n1 = n1 + jnp.where(x1o == sv1[0], jnp.maximum(n_top - 8.0, 0.0), 0.0)
        bad = bad | (jnp.sum(n1, axis=0, keepdims=True) != PEER_TOPK)
        n1_ref[h] = n1
        w1_ref[h] = jnp.exp(x1o - sv1[0]) * (1.0 / z)
        r2_ref[h] = r2.astype(BF16)
        p2_ref[h] = jnp.exp(x2o - sv2[0]).astype(BF16)
        return bad

    def head(h, carry):
        bad = head_fast(h)

        @pl.when(jnp.max(jnp.where(bad, 1, 0)) > 0)
        def _():
            head_exact(h)

        return carry

    def head_exact(h):
        sv1, si1 = _top16_rows(s_scr[2 * h], iota0)
        x2 = s_scr[2 * h + 1]
        r2 = jnp.full((N_KEYS, tt), float(N_KEYS - 1), F32)
        p2 = jnp.zeros((N_KEYS, tt), F32)
        sv2 = []
        for r in range(PEER_TOPK):
            m = jnp.max(x2, axis=0, keepdims=True)
            idx = jnp.min(jnp.where(x2 == m, iota0, N_KEYS), axis=0, keepdims=True)
            hit = iota0 == idx
            x2 = jnp.where(hit, NEG_INF, x2)
            sv2.append(m)
            r2 = jnp.where(hit, float(r), r2)
            p2 = jnp.where(hit, jnp.exp(m - sv2[0]), p2)
        e1 = [jnp.exp(v - sv1[0]) for v in sv1]
        sv2a = jnp.concatenate(sv2, axis=0)
        e2a = jnp.exp(sv2a - sv2[0])
        cand = jnp.concatenate([sv1[0] + sv2a] + [sv1[a] + sv2a[:8] for a in range(1, 16)], axis=0)
        egrid = jnp.concatenate([e1[0] * e2a] + [e1[a] * e2a[:8] for a in range(1, 16)], axis=0)
        cand = jnp.where(cvalid, cand, NEG_INF)
        sel = jnp.zeros((CAND_ROWS, tt), F32)
        for _ in range(PEER_TOPK):
            m = jnp.max(cand, axis=0, keepdims=True)
            f = jnp.min(jnp.where(cand == m, cflat, PEER_TOPK * PEER_TOPK), axis=0, keepdims=True)
            hit = cflat == f
            cand = jnp.where(hit, NEG_INF, cand)
            sel = jnp.where(hit, 1.0, sel)
        z = jnp.sum(sel * egrid, axis=0, keepdims=True)
        inv_z = 1.0 / z
        n1 = jnp.zeros((N_KEYS, tt), F32)
        w1 = jnp.zeros((N_KEYS, tt), F32)
        for a in range(PEER_TOPK):
            lo = 0 if a == 0 else 16 + 8 * (a - 1)
            hi = 16 if a == 0 else lo + 8
            n_a = jnp.sum(sel[lo:hi], axis=0, keepdims=True)
            hit = iota0 == si1[a]
            n1 = jnp.where(hit, n_a, n1)
            w1 = jnp.where(hit, e1[a] * inv_z, w1)
        n1_ref[h] = n1
        w1_ref[h] = w1
        r2_ref[h] = r2.astype(BF16)
        p2_ref[h] = p2.astype(BF16)

    lax.fori_loop(0, PEER_HEADS, head, 0)


def _peer_select(h2t, wpqt, sk):
    t = h2t.shape[1]
    tt = PEER_TT
    assert t % tt == 0
    tok = lambda i: (0, 0, i)
    shp = (PEER_HEADS, N_KEYS, t)
    return pl.pallas_call(
        _peer_select_kernel,
        grid=(t // tt,),
        in_specs=[
            pl.BlockSpec((D_MODEL, tt), lambda i: (0, i)),
            pl.BlockSpec(wpqt.shape, lambda i: (0, 0)),
            pl.BlockSpec(sk.shape, lambda i: (0, 0, 0)),
        ],
        out_specs=[pl.BlockSpec((PEER_HEADS, N_KEYS, tt), tok)] * 4,
        out_shape=[jax.ShapeDtypeStruct(shp, F32), jax.ShapeDtypeStruct(shp, F32),
                   jax.ShapeDtypeStruct(shp, BF16), jax.ShapeDtypeStruct(shp, BF16)],
        scratch_shapes=[pltpu.VMEM((2 * PEER_HEADS, N_KEYS, tt), F32)],
        compiler_params=_cparams(("arbitrary",)),
        name="peer_select",
    )(h2t, wpqt, sk)


PEER_TM = 512
PEER_TE = 2048


def _gelu_exact(x):
    return 0.5 * x * (1.0 + lax.erf(x * (1.0 / math.sqrt(2.0))))


def _peer_expert_kernel(xt_ref, u_ref, vt_ref, n1_ref, w1_ref, r2_ref, p2_ref, x2_ref, out_ref,
                        acc_ref, m_ref):
    j = pl.program_id(1)
    nc = PEER_TE // N_KEYS
    tm = xt_ref.shape[1]

    @pl.when(j == 0)
    def _():
        acc_ref[...] = jnp.zeros_like(acc_ref)

    s = jnp.dot(u_ref[...], xt_ref[...], preferred_element_type=F32)
    c0 = pl.multiple_of(j * nc, nc)
    n_rows = [n1_ref[h, pl.ds(c0, nc), :].astype(BF16) for h in range(PEER_HEADS)]
    w_rows = [w1_ref[h, pl.ds(c0, nc), :].astype(BF16) for h in range(PEER_HEADS)]
    for cc in range(nc):
        a = _gelu_exact(s[cc * N_KEYS:(cc + 1) * N_KEYS, :])
        g = jnp.zeros((N_KEYS, tm), BF16)
        for h in range(PEER_HEADS):
            nb = jnp.broadcast_to(n_rows[h][cc:cc + 1, :], (N_KEYS, tm))
            wb = jnp.broadcast_to(w_rows[h][cc:cc + 1, :], (N_KEYS, tm))
            g = g + jnp.where(r2_ref[h] < nb, p2_ref[h], jnp.zeros((), BF16)) * wb
        m_ref[cc * N_KEYS:(cc + 1) * N_KEYS, :] = a.astype(BF16) * g
    acc_ref[...] += jnp.dot(vt_ref[...], m_ref[...], preferred_element_type=F32)

    @pl.when(j == pl.num_programs(1) - 1)
    def _():
        out_ref[...] = x2_ref[...] + acc_ref[...].T


def _peer_experts(xt, u, vt, n1, w1, r2, p2, x2):
    d, t = xt.shape
    e = u.shape[0]
    tm, te = PEER_TM, PEER_TE
    assert t % tm == 0 and e % te == 0
    nblk = e // te
    fac = pl.BlockSpec((PEER_HEADS, N_KEYS, tm), lambda i, j: (0, 0, i))
    return pl.pallas_call(
        _peer_expert_kernel,
        grid=(t // tm, nblk),
        in_specs=[
            pl.BlockSpec((d, tm), lambda i, j: (0, i)),
            pl.BlockSpec((te, d), lambda i, j: (j, 0)),
            pl.BlockSpec((None, d, te), lambda i, j: (j, 0, 0)),
            fac, fac, fac, fac,
            pl.BlockSpec((tm, d), lambda i, j: (i, 0)),
        ],
        out_specs=pl.BlockSpec((tm, d), lambda i, j: (i, 0)),
        out_shape=jax.ShapeDtypeStruct((t, d), F32),
        scratch_shapes=[pltpu.VMEM((d, tm), F32),
                        pltpu.VMEM((te, tm), BF16)],
        compiler_params=_cparams(("arbitrary", "arbitrary")),
        name="peer_experts",
    )(xt, u, vt, n1, w1, r2, p2, x2)


def _peer(x2, h2t, w_pq, sub_keys, expert_u, expert_v):
    wpqt = w_pq.T.astype(BF16)
    sk = sub_keys.reshape(2 * PEER_HEADS, N_KEYS, PEER_HALF).astype(BF16)
    n1, w1, r2, p2 = _peer_select(h2t, wpqt, sk)
    vt = expert_v.astype(BF16).reshape(-1, PEER_TE, expert_v.shape[1]).transpose(0, 2, 1)
    return _peer_experts(h2t, expert_u.astype(BF16), vt, n1, w1, r2, p2, x2)


IN_TM = 256
C_U, C_Q, C_K, C_V, C_IQ, C_IK, C_IW, C_END = 0, 512, 1024, 1152, 1280, 1792, 1920, 2048
POOL_HALO = 16


def _pack_w_in(w_in):
    o = np.cumsum((0, POOL_W, N_HEADS * HEAD_DIM, HEAD_DIM, HEAD_DIM, IDX_HEADS * IDX_DIM, IDX_DIM, IDX_HEADS))
    z = lambda n: jnp.zeros((D_MODEL, n), w_in.dtype)
    return jnp.concatenate([w_in[:, o[0]:o[5]], w_in[:, o[5]:o[6]], z(C_IW - C_IK - IDX_DIM),
                            w_in[:, o[6]:o[7]], z(C_END - C_IW - IDX_HEADS)], axis=1).astype(BF16)


def _rope_tables(pos):
    def tab(half, reps):
        inv = ROPE_THETA ** (-jnp.arange(half, dtype=F32) / half)
        ang = pos.astype(F32)[:, None] * inv[None, :]
        c, s = jnp.cos(ang), jnp.sin(ang)
        return jnp.tile(jnp.concatenate([c, c], 1), (1, reps)), jnp.tile(jnp.concatenate([-s, s], 1), (1, reps))
    c128, s128 = tab(HEAD_DIM // 2, 1)
    c64, s64 = tab(IDX_DIM // 2, 2)
    return jnp.concatenate([c128, s128, c64, s64], axis=1)


def _rope128(x, tab):
    return x * tab[:, 0:128] + pltpu.roll(x, 64, 1) * tab[:, 128:256]


def _rope64x2(x, tab):
    lane = lax.broadcasted_iota(I32, x.shape, 1)
    partner = jnp.where((lane & 63) < 32, pltpu.roll(x, 96, 1), pltpu.roll(x, 32, 1))
    return x * tab[:, 256:384] + partner * tab[:, 384:512]


def _head_norm(x, g):
    return x * lax.rsqrt(jnp.mean(x * x, axis=-1, keepdims=True) + EPS) * g


def _project_block(x_ref, tab_ref, g1_ref, w_ref, qg_ref, kg_ref,
                   q_ref, k_ref, v_ref, kb_ref, vb_ref, iq_ref, ik_ref, ikb_ref, iw_ref, keys_major=False):
    x = x_ref[...]
    h = x * lax.rsqrt(jnp.mean(x * x, axis=-1, keepdims=True) + EPS) * g1_ref[...]
    z = jnp.dot(h.astype(BF16), w_ref[...], preferred_element_type=F32)
    tab = tab_ref[...]
    for hh in range(N_HEADS):
        sl = slice(C_Q + hh * HEAD_DIM, C_Q + (hh + 1) * HEAD_DIM)
        q_ref[:, hh * HEAD_DIM:(hh + 1) * HEAD_DIM] = _rope128(_head_norm(z[:, sl], qg_ref[...]), tab).astype(BF16)
    k = _rope128(_head_norm(z[:, C_K:C_V], kg_ref[...]), tab)
    v = z[:, C_V:C_IQ]
    k_ref[...] = k
    v_ref[...] = v
    kb_ref[...] = k.astype(BF16)
    vb_ref[...] = (v.T if keys_major else v).astype(BF16)
    for hp in range(IDX_HEADS // 2):
        sl = slice(C_IQ + hp * 128, C_IQ + (hp + 1) * 128)
        iq_ref[:, hp * 128:(hp + 1) * 128] = _rope64x2(z[:, sl], tab).astype(BF16)
    ik = _rope64x2(z[:, C_IK:C_IW], tab)[:, :IDX_DIM]
    ik_ref[...] = ik
    ikb_ref[...] = ik.astype(BF16)
    iw = z[:, C_IW:C_END] * (IDX_HEADS ** -0.5)
    iw_ref[...] = iw.T[:IDX_HEADS, :] if keys_major else iw[:, :IDX_HEADS]
    return z[:, C_U:C_Q]


def _pool_out(d_list, wp_ref, ps_ref, ya_ref, rows):
    for g in range(len(POOL_WINDOWS)):
        y = jnp.dot(d_list[g].astype(BF16), wp_ref[g], preferred_element_type=F32)
        ya_ref[rows, g * POOL_GW:(g + 1) * POOL_GW] = (y * ps_ref[:, g * POOL_GW:(g + 1) * POOL_GW]).astype(BF16)


def _inproj_prompt_kernel(x_ref, tab_ref, g1_ref, w_ref, qg_ref, kg_ref, wp_ref, ps_ref,
                          ya_ref, q_ref, k_ref, v_ref, kb_ref, vb_ref, iq_ref, ik_ref, ikb_ref, iw_ref,
                          pool_ref, ext_scr):
    tm = x_ref.shape[0]
    blk = pl.program_id(1)

    @pl.when(blk == 0)
    def _():
        ext_scr[0:POOL_HALO, :] = jnp.zeros((POOL_HALO, POOL_W), F32)

    u = _project_block(x_ref, tab_ref, g1_ref, w_ref, qg_ref, kg_ref,
                       q_ref, k_ref, v_ref, kb_ref, vb_ref, iq_ref, ik_ref, ikb_ref, iw_ref, keys_major=True)
    ext_scr[POOL_HALO:POOL_HALO + tm, :] = u
    pos = blk * tm + lax.broadcasted_iota(I32, (tm, 1), 0)
    d_list = []
    for g, w in enumerate(POOL_WINDOWS):
        sl = slice(g * POOL_GW, (g + 1) * POOL_GW)
        acc = u[:, sl]
        for j in range(1, w):
            acc = acc + ext_scr[POOL_HALO - j:POOL_HALO - j + tm, sl]
        cnt = jnp.minimum(pos + 1, w).astype(F32)
        d_list.append(acc / cnt - u[:, sl])
    _pool_out(d_list, wp_ref, ps_ref, ya_ref, slice(None))
    tail = ext_scr[tm:tm + POOL_HALO, :]
    pool_ref[...] = tail
    ext_scr[0:POOL_HALO, :] = tail


def _inproj_outs(n, lead):
    sd = lambda w, dt: jax.ShapeDtypeStruct(lead + (n, w), dt)
    return [sd(POOL_W, BF16), sd(N_HEADS * HEAD_DIM, BF16), sd(HEAD_DIM, F32), sd(HEAD_DIM, F32),
            sd(HEAD_DIM, BF16), sd(HEAD_DIM, BF16), sd(IDX_HEADS * IDX_DIM, BF16), sd(IDX_DIM, F32),
            sd(IDX_DIM, BF16), sd(IDX_HEADS, F32)]


def _inproj_prompt(x, tab, g1, w_p, qg, kg, wp, ps):
    b, s, d = x.shape
    tm = IN_TM
    assert s % tm == 0
    full = lambda a: pl.BlockSpec(a.shape, lambda bi, i: (0,) * a.ndim)
    rows = lambda w: pl.BlockSpec((None, tm, w), lambda bi, i: (bi, i, 0))
    cols = lambda h: pl.BlockSpec((None, h, tm), lambda bi, i: (bi, 0, i))
    outs = _inproj_outs(s, (b,)) + [jax.ShapeDtypeStruct((b, POOL_HALO, POOL_W), F32)]
    out_specs = [rows(o.shape[-1]) for o in outs[:-1]] + [
        pl.BlockSpec((None, POOL_HALO, POOL_W), lambda bi, i: (bi, 0, 0))]
    outs[5] = jax.ShapeDtypeStruct((b, HEAD_DIM, s), BF16)
    outs[9] = jax.ShapeDtypeStruct((b, IDX_HEADS, s), F32)
    out_specs[5], out_specs[9] = cols(HEAD_DIM), cols(IDX_HEADS)
    return pl.pallas_call(
        _inproj_prompt_kernel,
        grid=(b, s // tm),
        in_specs=[rows(d), pl.BlockSpec((tm, tab.shape[1]), lambda bi, i: (i, 0)),
                  full(g1), full(w_p), full(qg), full(kg), full(wp), full(ps)],
        out_specs=out_specs,
        out_shape=outs,
        scratch_shapes=[pltpu.VMEM((POOL_HALO + tm, POOL_W), F32)],
        compiler_params=_cparams(("arbitrary", "arbitrary")),
        name="inproj_prompt",
    )(x, tab, g1, w_p, qg, kg, wp, ps)


def _inproj_sample_kernel(x_ref, tab_ref, g1_ref, w_ref, qg_ref, kg_ref, wp_ref, ps_ref, st_ref,
                          ya_ref, q_ref, k_ref, v_ref, kb_ref, vb_ref, iq_ref, ik_ref, ikb_ref, iw_ref,
                          pool_ref, *, n_seq, n_new, past):
    u = _project_block(x_ref, tab_ref, g1_ref, w_ref, qg_ref, kg_ref,
                       q_ref, k_ref, v_ref, kb_ref, vb_ref, iq_ref, ik_ref, ikb_ref, iw_ref)
    hist = [st_ref[i] for i in range(POOL_STATE)] + [u[j * n_seq:(j + 1) * n_seq, :] for j in range(n_new)]
    for j in range(n_new):
        d_list = []
        for g, w in enumerate(POOL_WINDOWS):
            sl = slice(g * POOL_GW, (g + 1) * POOL_GW)
            acc = hist[POOL_STATE + j][:, sl]
            for i in range(1, w):
                acc = acc + hist[POOL_STATE + j - i][:, sl]
            d_list.append(acc / float(min(past + j + 1, w)) - hist[POOL_STATE + j][:, sl])
        _pool_out(d_list, wp_ref, ps_ref, ya_ref, slice(j * n_seq, (j + 1) * n_seq))
    for i in range(POOL_STATE):
        pool_ref[i] = hist[n_new + i]


def _inproj_sample(x_tm, tab, g1, w_p, qg, kg, wp, ps, st_tm, n_seq, n_new, past):
    n = x_tm.shape[0]
    outs = _inproj_outs(n, ()) + [jax.ShapeDtypeStruct((POOL_STATE, n_seq, POOL_W), F32)]
    return pl.pallas_call(
        functools.partial(_inproj_sample_kernel, n_seq=n_seq, n_new=n_new, past=past),
        out_shape=outs,
        compiler_params=pltpu.CompilerParams(vmem_limit_bytes=VMEM_LIMIT_BYTES),
        name="inproj_sample",
    )(x_tm, tab, g1, w_p, qg, kg, wp, ps, st_tm)


OUT_TM = 512


def _outproj_kernel(x_ref, ya_ref, yb_ref, wa_ref, wb_ref, g2_ref, x2_ref, h2t_ref):
    x2 = (x_ref[...] + jnp.dot(ya_ref[...], wa_ref[...], preferred_element_type=F32)
          + jnp.dot(yb_ref[...], wb_ref[...], preferred_element_type=F32))
    x2_ref[...] = x2
    h2 = x2 * lax.rsqrt(jnp.mean(x2 * x2, axis=-1, keepdims=True) + EPS) * g2_ref[...]
    h2t_ref[...] = h2.T.astype(BF16)


def _outproj(x, ya, yb, w_o, g2):
    t, d = x.shape
    tm = OUT_TM
    assert t % tm == 0
    wa = w_o[:POOL_W].astype(BF16)
    wb = w_o[POOL_W:].astype(BF16)
    rows = lambda w: pl.BlockSpec((tm, w), lambda i: (i, 0))
    full = lambda a: pl.BlockSpec(a.shape, lambda i: (0,) * a.ndim)
    return pl.pallas_call(
        _outproj_kernel,
        grid=(t // tm,),
        in_specs=[rows(d), rows(POOL_W), rows(yb.shape[1]), full(wa), full(wb), full(g2)],
        out_specs=[rows(d), pl.BlockSpec((d, tm), lambda i: (0, i))],
        out_shape=[jax.ShapeDtypeStruct((t, d), F32), jax.ShapeDtypeStruct((d, t), BF16)],
        compiler_params=_cparams(("arbitrary",)),
        name="outproj",
    )(x, ya, yb, wa, wb, g2)


DSA_TQ = 256
INT_MIN = -2 ** 31


def _order_key(score):
    b = pltpu.bitcast(score + 0.0, I32)
    return jnp.where(b < 0, b ^ 0x7FFFFFFF, b)


def _kth_largest_key(count_ge, k, shape):
    def bit_step(i, t_u):
        cand_u = t_u | (jnp.int32(1) << (31 - i))
        cnt = count_ge(cand_u ^ INT_MIN)
        return jnp.where(cnt >= k, cand_u, t_u)
    t_u = lax.fori_loop(0, 32, bit_step, jnp.zeros(shape, I32))
    return t_u ^ INT_MIN


def _dsa_prompt_kernel(iq_ref, iwt_ref, q_ref, ik_ref, k_ref, vt_ref, o_ref,
                       key_scr, bias_scr, tie_scr, carry_scr, *, k_sel):
    tq = DSA_TQ
    qb = pl.program_id(1)
    n_chunks = qb + 1
    krow = lax.broadcasted_iota(I32, (tq, tq), 0)
    qcol = lax.broadcasted_iota(I32, (tq, tq), 1)
    nt = (((1,), (1,)), ((), ()))

    def chunk_start(j):
        return pl.multiple_of(j * tq, tq)

    def score_chunk(j, c):
        ikc = ik_ref[0, pl.ds(chunk_start(j), tq), :]
        sc = jnp.zeros((tq, tq), F32)
        for h in range(IDX_HEADS):
            s = lax.dot_general(ikc, iq_ref[0, :, h * IDX_DIM:(h + 1) * IDX_DIM], nt,
                                preferred_element_type=F32)
            sc = sc + jnp.maximum(s, 0.0) * iwt_ref[0, h:h + 1, :]
        key = _order_key(sc)
        key = jnp.where((j < qb) | (krow <= qcol), key, INT_MIN)
        key_scr[pl.ds(chunk_start(j), tq), :] = key
        return c

    lax.fori_loop(0, n_chunks, score_chunk, 0)

    def count(pred):
        def body(j, acc):
            kc = key_scr[pl.ds(chunk_start(j), tq), :]
            m = jnp.where(pred(kc), 1, 0)
            return acc + jnp.sum(m.reshape(tq // SUBLANES, SUBLANES, tq), axis=0)
        acc = lax.fori_loop(0, n_chunks, body, jnp.zeros((SUBLANES, tq), I32))
        return jnp.sum(acc, axis=0, keepdims=True)

    thr = _kth_largest_key(lambda t: count(lambda kc: kc >= t), k_sel, (1, tq))
    n_gt = count(lambda kc: kc > thr)
    n_eq = count(lambda kc: (kc == thr) & (kc > INT_MIN))
    room = k_sel - n_gt
    ties_cut = jnp.max(n_eq - room) > 0

    carry_scr[...] = jnp.zeros(carry_scr.shape, F32)
    scale = HEAD_DIM ** -0.5

    def logits(j, h, bias):
        kch = k_ref[0, pl.ds(chunk_start(j), tq), :]
        qh = q_ref[0, :, h * HEAD_DIM:(h + 1) * HEAD_DIM]
        return lax.dot_general(kch, qh, nt, preferred_element_type=F32) * scale + bias

    def max_chunk(j, ms):
        kc = key_scr[pl.ds(chunk_start(j), tq), :]
        eq = (kc == thr) & (kc > INT_MIN)

        @pl.when(ties_cut)
        def _():
            tri = jnp.where(qcol <= krow, 1.0, 0.0).astype(BF16)
            eqf = jnp.where(eq, 1.0, 0.0)
            prefix = jnp.dot(tri, eqf.astype(BF16), preferred_element_type=F32) + carry_scr[...]
            tie_scr[...] = jnp.where(prefix <= room.astype(F32), 1.0, 0.0)
            carry_scr[...] = carry_scr[...] + jnp.sum(eqf, axis=0, keepdims=True)

        @pl.when(jnp.logical_not(ties_cut))
        def _():
            tie_scr[...] = jnp.ones(tie_scr.shape, F32)

        sel = (kc > thr) | (eq & (tie_scr[...] > 0.0))
        bias = jnp.where(sel, 0.0, NEG_INF)
        bias_scr[pl.ds(chunk_start(j), tq), :] = bias
        return tuple(jnp.maximum(ms[h], jnp.max(logits(j, h, bias), axis=0, keepdims=True))
                     for h in range(N_HEADS))

    ms = lax.fori_loop(0, n_chunks, max_chunk,
                       tuple(jnp.full((1, tq), NEG_INF, F32) for _ in range(N_HEADS)))

    def attend_chunk(j, carry):
        dens, nums = carry
        bias = bias_scr[pl.ds(chunk_start(j), tq), :]
        vtc = vt_ref[0, :, pl.ds(chunk_start(j), tq)]
        ps = [jnp.exp(logits(j, h, bias) - ms[h]) for h in range(N_HEADS)]
        dens = tuple(dens[h] + jnp.sum(ps[h], axis=0, keepdims=True) for h in range(N_HEADS))
        nums = tuple(nums[h] + jnp.dot(vtc, ps[h].astype(BF16), preferred_element_type=F32)
                     for h in range(N_HEADS))
        return dens, nums

    dens, nums = lax.fori_loop(
        0, n_chunks, attend_chunk,
        (tuple(jnp.zeros((1, tq), F32) for _ in range(N_HEADS)),
         tuple(jnp.zeros((HEAD_DIM, tq), F32) for _ in range(N_HEADS))))
    for h in range(N_HEADS):
        o_ref[0, :, h * HEAD_DIM:(h + 1) * HEAD_DIM] = (nums[h] / dens[h]).T.astype(o_ref.dtype)


def _dsa_prompt_pallas(q, k, vt, iq, ik, iwt):
    b, s, _ = q.shape
    tq = DSA_TQ
    assert s % tq == 0
    k_sel = min(TOPK_MAX, s // 4)
    blk = lambda w: pl.BlockSpec((1, tq, w), lambda bi, qi: (bi, qi, 0))
    seq = lambda w: pl.BlockSpec((1, s, w), lambda bi, qi: (bi, 0, 0))
    return pl.pallas_call(
        functools.partial(_dsa_prompt_kernel, k_sel=k_sel),
        grid=(b, s // tq),
        in_specs=[blk(IDX_HEADS * IDX_DIM),
                  pl.BlockSpec((1, IDX_HEADS, tq), lambda bi, qi: (bi, 0, qi)),
                  blk(N_HEADS * HEAD_DIM), seq(IDX_DIM), seq(HEAD_DIM),
                  pl.BlockSpec((1, HEAD_DIM, s), lambda bi, qi: (bi, 0, 0))],
        out_specs=blk(N_HEADS * HEAD_DIM),
        out_shape=jax.ShapeDtypeStruct((b, s, N_HEADS * HEAD_DIM), BF16),
        scratch_shapes=[
            pltpu.VMEM((s, tq), I32),
            pltpu.VMEM((s, tq), F32),
            pltpu.VMEM((tq, tq), F32),
            pltpu.VMEM((1, tq), F32),
        ],
        compiler_params=_cparams(("arbitrary", "arbitrary")),
        name="dsa_prompt",
    )(iq, iwt, q, ik, k, vt)


DEC_ROWS = 8
NEW_PAD = 128


def _dsa_sample_kernel(pt_ref, iq_ref, wsel_ref, q_ref, ikn_ref, kn_ref, vn_ref, cik_ref, ck_ref, cv_ref,
                       o_ref, ikbuf, kbuf, vbuf, sem, tie_scr, *, n_pages, n_new, k_sel):
    bi = pl.program_id(0)
    nb = pl.num_programs(0)
    slot = bi % 2
    n_past = n_pages * PAGE_SIZE
    nt = (((1,), (1,)), ((), ()))

    def page_copies(seq, sl):
        cps = []
        for p in range(n_pages):
            pg = pt_ref[seq, p]
            cps.append(pltpu.make_async_copy(cik_ref.at[pg], ikbuf.at[sl, p], sem.at[sl, 0]))
            cps.append(pltpu.make_async_copy(ck_ref.at[pg], kbuf.at[sl, p], sem.at[sl, 1]))
            cps.append(pltpu.make_async_copy(cv_ref.at[pg], vbuf.at[sl, p], sem.at[sl, 2]))
        return cps

    @pl.when(bi == 0)
    def _():
        for cp in page_copies(0, 0):
            cp.start()

    @pl.when(bi + 1 < nb)
    def _():
        for cp in page_copies(bi + 1, 1 - slot):
            cp.start()

    for cp in page_copies(bi, slot):
        cp.wait()

    iq = iq_ref[...]
    wsel = wsel_ref[...]
    ik_all = ikbuf[slot].reshape(n_past, IDX_DIM).astype(BF16)
    s_past = lax.dot_general(iq, ik_all, nt, preferred_element_type=F32)
    s_new = lax.dot_general(iq, ikn_ref[...], nt, preferred_element_type=F32)
    hi = lax.Precision.HIGHEST
    sc_past = jnp.dot(wsel, jnp.maximum(s_past, 0.0), precision=hi, preferred_element_type=F32)
    sc_new = jnp.dot(wsel, jnp.maximum(s_new, 0.0), precision=hi, preferred_element_type=F32)
    row_p = lax.broadcasted_iota(I32, (DEC_ROWS, n_past), 0)
    row_n = lax.broadcasted_iota(I32, (DEC_ROWS, NEW_PAD), 0)
    col_n = lax.broadcasted_iota(I32, (DEC_ROWS, NEW_PAD), 1)
    key_p = jnp.where(row_p < n_new, _order_key(sc_past), INT_MIN)
    key_n = jnp.where((row_n < n_new) & (col_n <= row_n), _order_key(sc_new), INT_MIN)

    def count(pred):
        one = lambda m: jnp.sum(jnp.where(m, 1, 0), axis=1, keepdims=True)
        return one(pred(key_p)) + one(pred(key_n))

    thr = _kth_largest_key(lambda t: count(lambda kc: kc >= t), k_sel, (DEC_ROWS, 1))
    n_gt = count(lambda kc: kc > thr)
    eq_p = (key_p == thr) & (key_p > INT_MIN)
    eq_n = (key_n == thr) & (key_n > INT_MIN)
    n_eq = (jnp.sum(jnp.where(eq_p, 1, 0), axis=1, keepdims=True)
            + jnp.sum(jnp.where(eq_n, 1, 0), axis=1, keepdims=True))
    room = k_sel - n_gt
    ties_cut = jnp.max(n_eq - room) > 0

    @pl.when(ties_cut)
    def _():
        r = lax.broadcasted_iota(I32, (LANES, LANES), 0)
        c = lax.broadcasted_iota(I32, (LANES, LANES), 1)
        tri = jnp.where(r <= c, 1.0, 0.0).astype(BF16)
        roomf = room.astype(F32)
        carry = jnp.zeros((DEC_ROWS, 1), F32)
        eqf_p = jnp.where(eq_p, 1.0, 0.0)
        for ch in range(n_past // LANES):
            e = eqf_p[:, ch * LANES:(ch + 1) * LANES]
            prefix = jnp.dot(e.astype(BF16), tri, preferred_element_type=F32) + carry
            tie_scr[:, ch * LANES:(ch + 1) * LANES] = jnp.where(prefix <= roomf, 1.0, 0.0)
            carry = carry + jnp.sum(e, axis=1, keepdims=True)
        e = jnp.where(eq_n, 1.0, 0.0)
        prefix = jnp.dot(e.astype(BF16), tri, preferred_element_type=F32) + carry
        tie_scr[:, n_past:n_past + NEW_PAD] = jnp.where(prefix <= roomf, 1.0, 0.0)

    @pl.when(jnp.logical_not(ties_cut))
    def _():
        tie_scr[...] = jnp.ones(tie_scr.shape, F32)

    sel_p = jnp.where((key_p > thr) | (eq_p & (tie_scr[:, 0:n_past] > 0.0)), 1.0, 0.0)
    sel_n = jnp.where((key_n > thr) | (eq_n & (tie_scr[:, n_past:n_past + NEW_PAD] > 0.0)), 1.0, 0.0)
    rr = lax.broadcasted_iota(I32, (N_HEADS * n_new, DEC_ROWS), 0)
    rc = lax.broadcasted_iota(I32, (N_HEADS * n_new, DEC_ROWS), 1)
    hit = rr == rc
    for hh in range(1, N_HEADS):
        hit = hit | (rr == rc + hh * n_new)
    rep = jnp.where(hit & (rc < n_new), 1.0, 0.0).astype(BF16)
    m_p = jnp.dot(rep, sel_p.astype(BF16), preferred_element_type=F32)
    m_n = jnp.dot(rep, sel_n.astype(BF16), preferred_element_type=F32)

    q = q_ref[...]
    scale = HEAD_DIM ** -0.5
    k_all = kbuf[slot].reshape(n_past, HEAD_DIM).astype(BF16)
    lg_p = jnp.where(m_p > 0.5, lax.dot_general(q, k_all, nt, preferred_element_type=F32) * scale, NEG_INF)
    lg_n = jnp.where(m_n > 0.5, lax.dot_general(q, kn_ref[...], nt, preferred_element_type=F32) * scale,
                     NEG_INF)
    mx = jnp.maximum(jnp.max(lg_p, axis=1, keepdims=True), jnp.max(lg_n, axis=1, keepdims=True))
    p_p = jnp.exp(lg_p - mx)
    p_n = jnp.exp(lg_n - mx)
    den = jnp.sum(p_p, axis=1, keepdims=True) + jnp.sum(p_n, axis=1, keepdims=True)
    v_all = vbuf[slot].reshape(n_past, HEAD_DIM).astype(BF16)
    num = (jnp.dot(p_p.astype(BF16), v_all, preferred_element_type=F32)
           + jnp.dot(p_n.astype(BF16), vn_ref[...], preferred_element_type=F32))
    o_ref[...] = num / den


def _dsa_sample_pallas(q, kb, vb, iq, ikb, iw, cache_k, cache_v, cache_ik, page_table):
    db, ds, _ = q.shape
    n_pages = page_table.shape[1]
    n_past = n_pages * PAGE_SIZE
    assert ds <= DEC_ROWS and ds <= NEW_PAD
    k_sel = min(TOPK_MAX, (n_past + ds) // 4)
    heads_first = lambda a, nh: a.reshape(db, ds, nh, -1).transpose(0, 2, 1, 3).reshape(db, nh * ds, -1)
    iq_r = heads_first(iq, IDX_HEADS)
    q_r = heads_first(q, N_HEADS)
    wsel = jnp.einsum('bjh,jk->bjhk', iw, jnp.eye(ds, dtype=F32)).reshape(db, ds, IDX_HEADS * ds)
    wsel = jnp.pad(wsel, ((0, 0), (0, DEC_ROWS - ds), (0, 0)))
    pad_new = lambda a: jnp.pad(a, ((0, 0), (0, NEW_PAD - ds), (0, 0)))
    per_seq = lambda a: pl.BlockSpec((None,) + a.shape[1:], lambda i, pt: (i, 0, 0))
    any_spec = pl.BlockSpec(memory_space=pl.ANY)
    ins = (iq_r, wsel, q_r, pad_new(ikb), pad_new(kb), pad_new(vb))
    out = pl.pallas_call(
        functools.partial(_dsa_sample_kernel, n_pages=n_pages, n_new=ds, k_sel=k_sel),
        grid_spec=pltpu.PrefetchScalarGridSpec(
            num_scalar_prefetch=1,
            grid=(db,),
            in_specs=[per_seq(a) for a in ins] + [any_spec, any_spec, any_spec],
            out_specs=pl.BlockSpec((None, N_HEADS * ds, HEAD_DIM), lambda i, pt: (i, 0, 0)),
            scratch_shapes=[
                pltpu.VMEM((2, n_pages, PAGE_SIZE, IDX_DIM), F32),
                pltpu.VMEM((2, n_pages, PAGE_SIZE, HEAD_DIM), F32),
                pltpu.VMEM((2, n_pages, PAGE_SIZE, HEAD_DIM), F32),
                pltpu.SemaphoreType.DMA((2, 3)),
                pltpu.VMEM((DEC_ROWS, n_past + NEW_PAD), F32),
            ],
        ),
        out_shape=jax.ShapeDtypeStruct((db, N_HEADS * ds, HEAD_DIM), F32),
        compiler_params=_cparams(("arbitrary",)),
        name="dsa_sample",
    )(page_table, *ins, cache_ik, cache_k, cache_v)
    return out.reshape(db, N_HEADS, ds, HEAD_DIM).transpose(0, 2, 1, 3).reshape(db, ds, N_HEADS * HEAD_DIM)


def kernel(x_prompt, x_sample, cache_k, cache_v, cache_ik, state_pool, page_table, norm1_g, w_in,
           q_norm_g, k_norm_g, w_pool, pool_scale, w_o, norm2_g, w_pq, sub_keys, expert_u, expert_v):
    b, s, d = x_prompt.shape
    db, ds, _ = x_sample.shape
    past = page_table.shape[1] * PAGE_SIZE
    w_p = _pack_w_in(w_in[0])
    row = lambda a: a[0][None, :]
    shared = (row(norm1_g), w_p, row(q_norm_g), row(k_norm_g), w_pool[0].astype(BF16), row(pool_scale))

    (ya, q, k1, v1, kb, vbt, iq, ik1, ikb, iwt, pool_p) = _inproj_prompt(
        x_prompt, _rope_tables(jnp.arange(s)), *shared)
    yb = _dsa_prompt_pallas(q, kb, vbt, iq, ikb, iwt)

    x_tm = x_sample.transpose(1, 0, 2).reshape(ds * db, d)
    tab_s = _rope_tables(past + jnp.repeat(jnp.arange(ds), db))
    (ya_s, q_s, k2, v2, kb_s, vb_s, iq_s, ik2, ikb_s, iw_s, pool_s) = _inproj_sample(
        x_tm, tab_s, *shared, state_pool[0].transpose(1, 0, 2), db, ds, past)
    bm = lambda a: a.reshape(ds, db, -1).transpose(1, 0, 2)
    k2, v2, ik2 = bm(k2), bm(v2), bm(ik2)
    yb_s = _dsa_sample_pallas(bm(q_s), bm(kb_s), bm(vb_s), bm(iq_s), bm(ikb_s), bm(iw_s),
                              cache_k[0], cache_v[0], cache_ik[0], page_table).astype(BF16)

    n_p, n_s = b * s, db * ds
    t_pad = -(-(n_p + n_s) // PEER_TM) * PEER_TM
    cat = lambda p, q_: jnp.pad(jnp.concatenate([p.reshape(n_p, -1), q_.reshape(n_s, -1)], axis=0),
                                ((0, t_pad - n_p - n_s), (0, 0)))
    x2, h2t = _outproj(cat(x_prompt, x_sample), cat(ya, bm(ya_s)), cat(yb, yb_s), w_o[0], row(norm2_g))
    y = _peer(x2, h2t, w_pq[0], sub_keys[0], expert_u[0], expert_v[0])
    st = lambda a: a[None]
    return (y[:n_p].reshape(b, s, d), y[n_p:n_p + n_s].reshape(db, ds, d),
            st(k1), st(v1), st(ik1), st(pool_p[:, 1:]),
            st(k2), st(v2), st(ik2), st(pool_s.transpose(1, 0, 2)))
```

```python
import functools
import math

import jax
import jax.numpy as jnp
import numpy as np
from jax import lax
from jax.experimental import pallas as pl
from jax.experimental.pallas import tpu as pltpu

F32 = jnp.float32
BF16 = jnp.bfloat16
I32 = jnp.int32

LANES = 128
SUBLANES = 8
VMEM_LIMIT_BYTES = 56 * 1024 * 1024

D_MODEL = 1024
POOL_W = 512
POOL_WINDOWS = (2, 4, 8, 16)
POOL_GW = 128
POOL_STATE = 15
HEAD_DIM = 128
N_HEADS = 4
IDX_HEADS = 8
IDX_DIM = 64
TOPK_MAX = 256
ROPE_THETA = 10000.0
PAGE_SIZE = 128
N_KEYS = 128
PEER_HEADS = 8
PEER_TOPK = 16
PEER_HALF = 128
EPS = 1e-6
NEG_INF = float("-inf")


def _cparams(sem):
    return pltpu.CompilerParams(dimension_semantics=sem, vmem_limit_bytes=VMEM_LIMIT_BYTES)


PEER_TT = 512
CAND_ROWS = 16 + 8 * 15


def _top16_rows(x, iota0):
    vals, idxs = [], []
    for _ in range(PEER_TOPK):
        m = jnp.max(x, axis=0, keepdims=True)
        idx = jnp.min(jnp.where(x == m, iota0, N_KEYS), axis=0, keepdims=True)
        x = jnp.where(iota0 == idx, NEG_INF, x)
        vals.append(m)
        idxs.append(idx)
    return vals, idxs


def _peer_select_kernel(h2t_ref, wpqt_ref, sk_ref, n1_ref, w1_ref, r2_ref, p2_ref, s_scr):
    tt = h2t_ref.shape[1]
    qt = jnp.dot(wpqt_ref[...], h2t_ref[...], preferred_element_type=F32).astype(BF16)
    for hc in range(2 * PEER_HEADS):
        s_scr[hc] = jnp.dot(sk_ref[hc], qt[hc * PEER_HALF:(hc + 1) * PEER_HALF, :],
                            preferred_element_type=F32)

    iota0 = lax.broadcasted_iota(I32, (N_KEYS, tt), 0)
    rows = lax.broadcasted_iota(I32, (CAND_ROWS, tt), 0)
    ca = jnp.where(rows < 16, 0, ((rows - 16) >> 3) + 1)
    cb = jnp.where(rows < 16, rows, (rows - 16) & 7)
    cvalid = (ca + 1) * (cb + 1) <= PEER_TOPK
    cflat = ca * PEER_TOPK + cb

    n_cand = sum(1 for a in range(16) for b in range(16) if (a + 1) * (b + 1) <= PEER_TOPK)

    def head_fast(h):
        x1o = s_scr[2 * h]
        x2o = s_scr[2 * h + 1]
        sv1, sv2 = [], []
        x1 = x1o
        for _ in range(PEER_TOPK):
            m = jnp.max(x1, axis=0, keepdims=True)
            x1 = jnp.where(x1 == m, NEG_INF, x1)
            sv1.append(m)
        x2 = x2o
        r2 = jnp.full((N_KEYS, tt), float(N_KEYS - 1), F32)
        for r in range(PEER_TOPK):
            m = jnp.max(x2, axis=0, keepdims=True)
            hit = x2 == m
            x2 = jnp.where(hit, NEG_INF, x2)
            r2 = jnp.where(hit, float(r), r2)
            sv2.append(m)
        gone = lambda x: jnp.sum(jnp.where(x == NEG_INF, 1.0, 0.0), axis=0, keepdims=True)
        bad = (gone(x1) != PEER_TOPK) | (gone(x2) != PEER_TOPK)
        sv2a = jnp.concatenate(sv2, axis=0)
        cand = jnp.concatenate([sv1[0] + sv2a] + [sv1[a] + sv2a[:8] for a in range(1, 16)], axis=0)
        cand = jnp.where(cvalid, cand, NEG_INF)
        fv = []
        for _ in range(PEER_TOPK):
            m = jnp.max(cand, axis=0, keepdims=True)
            cand = jnp.where(cand == m, NEG_INF, cand)
            fv.append(m)
        bad = bad | (gone(cand) != PEER_TOPK + CAND_ROWS - n_cand)
        tau = fv[PEER_TOPK - 1]
        z = jnp.sum(jnp.exp(jnp.concatenate(fv, axis=0) - fv[0]), axis=0, keepdims=True)
        n1 = jnp.zeros((N_KEYS, tt), F32)
        for b in range(8):
            n1 = n1 + jnp.where(x1o + sv2[b] >= tau, 1.0, 0.0)
        n_top = jnp.sum(jnp.where(sv1[0] + sv2a >= tau, 1.0, 0.0), axis=0, keepdims=True)
        n1 = n1 + jnp.where(x1o == sv1[0], jnp.maximum(n_top - 8.0, 0.0), 0.0)
        bad = bad | (jnp.sum(n1, axis=0, keepdims=True) != PEER_TOPK)
        n1_ref[h] = n1
        w1_ref[h] = jnp.exp(x1o - sv1[0]) * (1.0 / z)
        r2_ref[h] = r2.astype(BF16)
        p2_ref[h] = jnp.exp(x2o - sv2[0]).astype(BF16)
        return bad

    def head(h, carry):
        bad = head_fast(h)

        @pl.when(jnp.max(jnp.where(bad, 1, 0)) > 0)
        def _():
            head_exact(h)

        return carry

    def head_exact(h):
        sv1, si1 = _top16_rows(s_scr[2 * h], iota0)
        x2 = s_scr[2 * h + 1]
        r2 = jnp.full((N_KEYS, tt), float(N_KEYS - 1), F32)
        p2 = jnp.zeros((N_KEYS, tt), F32)
        sv2 = []
        for r in range(PEER_TOPK):
            m = jnp.max(x2, axis=0, keepdims=True)
            idx = jnp.min(jnp.where(x2 == m, iota0, N_KEYS), axis=0, keepdims=True)
            hit = iota0 == idx
            x2 = jnp.where(hit, NEG_INF, x2)
            sv2.append(m)
            r2 = jnp.where(hit, float(r), r2)
            p2 = jnp.where(hit, jnp.exp(m - sv2[0]), p2)
        e1 = [jnp.exp(v - sv1[0]) for v in sv1]
        sv2a = jnp.concatenate(sv2, axis=0)
        e2a = jnp.exp(sv2a - sv2[0])
        cand = jnp.concatenate([sv1[0] + sv2a] + [sv1[a] + sv2a[:8] for a in range(1, 16)], axis=0)
        egrid = jnp.concatenate([e1[0] * e2a] + [e1[a] * e2a[:8] for a in range(1, 16)], axis=0)
        cand = jnp.where(cvalid, cand, NEG_INF)
        sel = jnp.zeros((CAND_ROWS, tt), F32)
        for _ in range(PEER_TOPK):
            m = jnp.max(cand, axis=0, keepdims=True)
            f = jnp.min(jnp.where(cand == m, cflat, PEER_TOPK * PEER_TOPK), axis=0, keepdims=True)
            hit = cflat == f
            cand = jnp.where(hit, NEG_INF, cand)
            sel = jnp.where(hit, 1.0, sel)
        z = jnp.sum(sel * egrid, axis=0, keepdims=True)
        inv_z = 1.0 / z
        n1 = jnp.zeros((N_KEYS, tt), F32)
        w1 = jnp.zeros((N_KEYS, tt), F32)
        for a in range(PEER_TOPK):
            lo = 0 if a == 0 else 16 + 8 * (a - 1)
            hi = 16 if a == 0 else lo + 8
            n_a = jnp.sum(sel[lo:hi], axis=0, keepdims=True)
            hit = iota0 == si1[a]
            n1 = jnp.where(hit, n_a, n1)
            w1 = jnp.where(hit, e1[a] * inv_z, w1)
        n1_ref[h] = n1
        w1_ref[h] = w1
        r2_ref[h] = r2.astype(BF16)
        p2_ref[h] = p2.astype(BF16)

    lax.fori_loop(0, PEER_HEADS, head, 0)


def _peer_select(h2t, wpqt, sk):
    t = h2t.shape[1]
    tt = PEER_TT
    assert t % tt == 0
    tok = lambda i: (0, 0, i)
    shp = (PEER_HEADS, N_KEYS, t)
    return pl.pallas_call(
        _peer_select_kernel,
        grid=(t // tt,),
        in_specs=[
            pl.BlockSpec((D_MODEL, tt), lambda i: (0, i)),
            pl.BlockSpec(wpqt.shape, lambda i: (0, 0)),
            pl.BlockSpec(sk.shape, lambda i: (0, 0, 0)),
        ],
        out_specs=[pl.BlockSpec((PEER_HEADS, N_KEYS, tt), tok)] * 4,
        out_shape=[jax.ShapeDtypeStruct(shp, F32), jax.ShapeDtypeStruct(shp, F32),
                   jax.ShapeDtypeStruct(shp, BF16), jax.ShapeDtypeStruct(shp, BF16)],
        scratch_shapes=[pltpu.VMEM((2 * PEER_HEADS, N_KEYS, tt), F32)],
        compiler_params=_cparams(("arbitrary",)),
        name="peer_select",
    )(h2t, wpqt, sk)


PEER_TM = 512
PEER_TE = 2048


def _gelu_exact(x):
    return 0.5 * x * (1.0 + lax.erf(x * (1.0 / math.sqrt(2.0))))


def _peer_expert_kernel(xt_ref, u_ref, vt_ref, n1_ref, w1_ref, r2_ref, p2_ref, x2_ref, out_ref, outs_ref,
                        acc_ref, m_ref):
    j = pl.program_id(1)
    nc = PEER_TE // N_KEYS
    tm = xt_ref.shape[1]

    @pl.when(j == 0)
    def _():
        acc_ref[...] = jnp.zeros_like(acc_ref)

    s = jnp.dot(u_ref[...], xt_ref[...], preferred_element_type=F32)
    c0 = pl.multiple_of(j * nc, nc)
    n_rows = [n1_ref[h, pl.ds(c0, nc), :].astype(BF16) for h in range(PEER_HEADS)]
    w_rows = [w1_ref[h, pl.ds(c0, nc), :].astype(BF16) for h in range(PEER_HEADS)]
    for cc in range(nc):
        a = _gelu_exact(s[cc * N_KEYS:(cc + 1) * N_KEYS, :])
        g = jnp.zeros((N_KEYS, tm), BF16)
        for h in range(PEER_HEADS):
            nb = jnp.broadcast_to(n_rows[h][cc:cc + 1, :], (N_KEYS, tm))
            wb = jnp.broadcast_to(w_rows[h][cc:cc + 1, :], (N_KEYS, tm))
            g = g + jnp.where(r2_ref[h] < nb, p2_ref[h], jnp.zeros((), BF16)) * wb
        m_ref[cc * N_KEYS:(cc + 1) * N_KEYS, :] = a.astype(BF16) * g
    acc_ref[...] += jnp.dot(vt_ref[...], m_ref[...], preferred_element_type=F32)

    last = j == pl.num_programs(1) - 1
    tail = pl.program_id(0) == pl.num_programs(0) - 1

    @pl.when(last & jnp.logical_not(tail))
    def _():
        out_ref[...] = x2_ref[...] + acc_ref[...].T

    @pl.when(last & tail)
    def _():
        outs_ref[...] = x2_ref[...] + acc_ref[...].T


def _peer_experts(xt, u, vt, n1, w1, r2, p2, x2):
    d, t = xt.shape
    e = u.shape[0]
    tm, te = PEER_TM, PEER_TE
    assert t % tm == 0 and e % te == 0
    nblk = e // te
    ntb = t // tm
    fac = pl.BlockSpec((PEER_HEADS, N_KEYS, tm), lambda i, j: (0, 0, i))
    return pl.pallas_call(
        _peer_expert_kernel,
        grid=(ntb, nblk),
        in_specs=[
            pl.BlockSpec((d, tm), lambda i, j: (0, i)),
            pl.BlockSpec((te, d), lambda i, j: (j, 0)),
            pl.BlockSpec((None, d, te), lambda i, j: (j, 0, 0)),
            fac, fac, fac, fac,
            pl.BlockSpec((tm, d), lambda i, j: (i, 0)),
        ],
        out_specs=[pl.BlockSpec((tm, d), lambda i, j: (jnp.minimum(i, ntb - 2), 0)),
                   pl.BlockSpec((tm, d), lambda i, j: (0, 0))],
        out_shape=[jax.ShapeDtypeStruct((t - tm, d), F32), jax.ShapeDtypeStruct((tm, d), F32)],
        scratch_shapes=[pltpu.VMEM((d, tm), F32),
                        pltpu.VMEM((te, tm), BF16)],
        compiler_params=_cparams(("arbitrary", "arbitrary")),
        name="peer_experts",
    )(xt, u, vt, n1, w1, r2, p2, x2)


def _peer(x2, h2t, w_pq, sub_keys, expert_u, expert_v):
    wpqt = w_pq.T.astype(BF16)
    sk = sub_keys.reshape(2 * PEER_HEADS, N_KEYS, PEER_HALF).astype(BF16)
    n1, w1, r2, p2 = _peer_select(h2t, wpqt, sk)
    vt = expert_v.astype(BF16).reshape(-1, PEER_TE, expert_v.shape[1]).transpose(0, 2, 1)
    return _peer_experts(h2t, expert_u.astype(BF16), vt, n1, w1, r2, p2, x2)


IN_TM = 256
C_U, C_Q, C_K, C_V, C_IQ, C_IK, C_IW, C_END = 0, 512, 1024, 1152, 1280, 1792, 1920, 2048
POOL_HALO = 16


def _pack_w_in(w_in):
    o = np.cumsum((0, POOL_W, N_HEADS * HEAD_DIM, HEAD_DIM, HEAD_DIM, IDX_HEADS * IDX_DIM, IDX_DIM, IDX_HEADS))
    z = lambda n: jnp.zeros((D_MODEL, n), w_in.dtype)
    return jnp.concatenate([w_in[:, o[0]:o[5]], w_in[:, o[5]:o[6]], z(C_IW - C_IK - IDX_DIM),
                            w_in[:, o[6]:o[7]], z(C_END - C_IW - IDX_HEADS)], axis=1).astype(BF16)


def _rope_tables(pos):
    def tab(half, reps):
        inv = ROPE_THETA ** (-jnp.arange(half, dtype=F32) / half)
        ang = pos.astype(F32)[:, None] * inv[None, :]
        c, s = jnp.cos(ang), jnp.sin(ang)
        return jnp.tile(jnp.concatenate([c, c], 1), (1, reps)), jnp.tile(jnp.concatenate([-s, s], 1), (1, reps))
    c128, s128 = tab(HEAD_DIM // 2, 1)
    c64, s64 = tab(IDX_DIM // 2, 2)
    return jnp.concatenate([c128, s128, c64, s64], axis=1)


def _rope128(x, tab):
    return x * tab[:, 0:128] + pltpu.roll(x, 64, 1) * tab[:, 128:256]


def _rope64x2(x, tab):
    lane = lax.broadcasted_iota(I32, x.shape, 1)
    partner = jnp.where((lane & 63) < 32, pltpu.roll(x, 96, 1), pltpu.roll(x, 32, 1))
    return x * tab[:, 256:384] + partner * tab[:, 384:512]


def _head_norm(x, g):
    return x * lax.rsqrt(jnp.mean(x * x, axis=-1, keepdims=True) + EPS) * g


def _project_block(x_ref, tab_ref, g1_ref, w_ref, qg_ref, kg_ref,
                   q_ref, k_ref, v_ref, kb_ref, vb_ref, iq_ref, ik_ref, ikb_ref, iw_ref, keys_major=False):
    x = x_ref[...]
    h = x * lax.rsqrt(jnp.mean(x * x, axis=-1, keepdims=True) + EPS) * g1_ref[...]
    z = jnp.dot(h.astype(BF16), w_ref[...], preferred_element_type=F32)
    tab = tab_ref[...]
    for hh in range(N_HEADS):
        sl = slice(C_Q + hh * HEAD_DIM, C_Q + (hh + 1) * HEAD_DIM)
        q_ref[:, hh * HEAD_DIM:(hh + 1) * HEAD_DIM] = _rope128(_head_norm(z[:, sl], qg_ref[...]), tab).astype(BF16)
    k = _rope128(_head_norm(z[:, C_K:C_V], kg_ref[...]), tab)
    v = z[:, C_V:C_IQ]
    k_ref[...] = k
    v_ref[...] = v
    kb_ref[...] = k.astype(BF16)
    vb_ref[...] = (v.T if keys_major else v).astype(BF16)
    for hp in range(IDX_HEADS // 2):
        sl = slice(C_IQ + hp * 128, C_IQ + (hp + 1) * 128)
        iq_ref[:, hp * 128:(hp + 1) * 128] = _rope64x2(z[:, sl], tab).astype(BF16)
    ik = _rope64x2(z[:, C_IK:C_IW], tab)[:, :IDX_DIM]
    ik_ref[...] = ik
    ikb_ref[...] = ik.astype(BF16)
    iw = z[:, C_IW:C_END] * (IDX_HEADS ** -0.5)
    iw_ref[...] = iw.T[:IDX_HEADS, :] if keys_major else iw[:, :IDX_HEADS]
    return z[:, C_U:C_Q]


def _pool_out(d_list, wp_ref, ps_ref, ya_ref, rows):
    for g in range(len(POOL_WINDOWS)):
        y = jnp.dot(d_list[g].astype(BF16), wp_ref[g], preferred_element_type=F32)
        ya_ref[rows, g * POOL_GW:(g + 1) * POOL_GW] = (y * ps_ref[:, g * POOL_GW:(g + 1) * POOL_GW]).astype(BF16)


def _inproj_prompt_kernel(x_ref, tab_ref, g1_ref, w_ref, qg_ref, kg_ref, wp_ref, ps_ref,
                          ya_ref, q_ref, k_ref, v_ref, kb_ref, vb_ref, iq_ref, ik_ref, ikb_ref, iw_ref,
                          pool_ref, ext_scr):
    tm = x_ref.shape[0]
    blk = pl.program_id(1)

    @pl.when(blk == 0)
    def _():
        ext_scr[0:POOL_HALO, :] = jnp.zeros((POOL_HALO, POOL_W), F32)

    u = _project_block(x_ref, tab_ref, g1_ref, w_ref, qg_ref, kg_ref,
                       q_ref, k_ref, v_ref, kb_ref, vb_ref, iq_ref, ik_ref, ikb_ref, iw_ref, keys_major=True)
    ext_scr[POOL_HALO:POOL_HALO + tm, :] = u
    pos = blk * tm + lax.broadcasted_iota(I32, (tm, 1), 0)
    d_list = []
    for g, w in enumerate(POOL_WINDOWS):
        sl = slice(g * POOL_GW, (g + 1) * POOL_GW)
        acc = u[:, sl]
        for j in range(1, w):
            acc = acc + ext_scr[POOL_HALO - j:POOL_HALO - j + tm, sl]
        cnt = jnp.minimum(pos + 1, w).astype(F32)
        d_list.append(acc / cnt - u[:, sl])
    _pool_out(d_list, wp_ref, ps_ref, ya_ref, slice(None))
    tail = ext_scr[tm:tm + POOL_HALO, :]
    pool_ref[...] = tail
    ext_scr[0:POOL_HALO, :] = tail


def _inproj_outs(n, lead):
    sd = lambda w, dt: jax.ShapeDtypeStruct(lead + (n, w), dt)
    return [sd(POOL_W, BF16), sd(N_HEADS * HEAD_DIM, BF16), sd(HEAD_DIM, F32), sd(HEAD_DIM, F32),
            sd(HEAD_DIM, BF16), sd(HEAD_DIM, BF16), sd(IDX_HEADS * IDX_DIM, BF16), sd(IDX_DIM, F32),
            sd(IDX_DIM, BF16), sd(IDX_HEADS, F32)]


def _inproj_prompt(x, tab, g1, w_p, qg, kg, wp, ps):
    b, s, d = x.shape
    tm = IN_TM
    assert s % tm == 0
    full = lambda a: pl.BlockSpec(a.shape, lambda bi, i: (0,) * a.ndim)
    rows = lambda w: pl.BlockSpec((None, tm, w), lambda bi, i: (bi, i, 0))
    cols = lambda h: pl.BlockSpec((None, h, tm), lambda bi, i: (bi, 0, i))
    outs = _inproj_outs(s, (b,)) + [jax.ShapeDtypeStruct((b, POOL_HALO, POOL_W), F32)]
    out_specs = [rows(o.shape[-1]) for o in outs[:-1]] + [
        pl.BlockSpec((None, POOL_HALO, POOL_W), lambda bi, i: (bi, 0, 0))]
    outs[5] = jax.ShapeDtypeStruct((b, HEAD_DIM, s), BF16)
    outs[9] = jax.ShapeDtypeStruct((b, IDX_HEADS, s), F32)
    out_specs[5], out_specs[9] = cols(HEAD_DIM), cols(IDX_HEADS)
    return pl.pallas_call(
        _inproj_prompt_kernel,
        grid=(b, s // tm),
        in_specs=[rows(d), pl.BlockSpec((tm, tab.shape[1]), lambda bi, i: (i, 0)),
                  full(g1), full(w_p), full(qg), full(kg), full(wp), full(ps)],
        out_specs=out_specs,
        out_shape=outs,
        scratch_shapes=[pltpu.VMEM((POOL_HALO + tm, POOL_W), F32)],
        compiler_params=_cparams(("arbitrary", "arbitrary")),
        name="inproj_prompt",
    )(x, tab, g1, w_p, qg, kg, wp, ps)


def _inproj_sample_kernel(x_ref, tab_ref, g1_ref, w_ref, qg_ref, kg_ref, wp_ref, ps_ref, st_ref,
                          ya_ref, q_ref, k_ref, v_ref, kb_ref, vb_ref, iq_ref, ik_ref, ikb_ref, iw_ref,
                          pool_ref, *, n_seq, n_new, past):
    u = _project_block(x_ref, tab_ref, g1_ref, w_ref, qg_ref, kg_ref,
                       q_ref, k_ref, v_ref, kb_ref, vb_ref, iq_ref, ik_ref, ikb_ref, iw_ref)
    hist = [st_ref[i] for i in range(POOL_STATE)] + [u[j * n_seq:(j + 1) * n_seq, :] for j in range(n_new)]
    for j in range(n_new):
        d_list = []
        for g, w in enumerate(POOL_WINDOWS):
            sl = slice(g * POOL_GW, (g + 1) * POOL_GW)
            acc = hist[POOL_STATE + j][:, sl]
            for i in range(1, w):
                acc = acc + hist[POOL_STATE + j - i][:, sl]
            d_list.append(acc / float(min(past + j + 1, w)) - hist[POOL_STATE + j][:, sl])
        _pool_out(d_list, wp_ref, ps_ref, ya_ref, slice(j * n_seq, (j + 1) * n_seq))
    for i in range(POOL_STATE):
        pool_ref[i] = hist[n_new + i]


def _inproj_sample(x_tm, tab, g1, w_p, qg, kg, wp, ps, st_tm, n_seq, n_new, past):
    n = x_tm.shape[0]
    outs = _inproj_outs(n, ()) + [jax.ShapeDtypeStruct((POOL_STATE, n_seq, POOL_W), F32)]
    return pl.pallas_call(
        functools.partial(_inproj_sample_kernel, n_seq=n_seq, n_new=n_new, past=past),
        out_shape=outs,
        compiler_params=pltpu.CompilerParams(vmem_limit_bytes=VMEM_LIMIT_BYTES),
        name="inproj_sample",
    )(x_tm, tab, g1, w_p, qg, kg, wp, ps, st_tm)


OUT_TM = 512


def _outproj_kernel(xp_ref, yap_ref, ybp_ref, xs_ref, yas_ref, ybs_ref, wa_ref, wb_ref, g2_ref,
                    x2_ref, h2t_ref, *, n_prompt_blocks):
    def rows(x_ref, ya_ref, yb_ref):
        x2 = (x_ref[...] + jnp.dot(ya_ref[...], wa_ref[...], preferred_element_type=F32)
              + jnp.dot(yb_ref[...], wb_ref[...], preferred_element_type=F32))
        x2_ref[...] = x2
        h2 = x2 * lax.rsqrt(jnp.mean(x2 * x2, axis=-1, keepdims=True) + EPS) * g2_ref[...]
        h2t_ref[...] = h2.T.astype(BF16)

    @pl.when(pl.program_id(0) < n_prompt_blocks)
    def _():
        rows(xp_ref, yap_ref, ybp_ref)

    @pl.when(pl.program_id(0) == n_prompt_blocks)
    def _():
        rows(xs_ref, yas_ref, ybs_ref)


def _outproj(xp, yap, ybp, xs, yas, ybs, w_o, g2):
    n_p, d = xp.shape
    tm = OUT_TM
    assert n_p % tm == 0 and xs.shape[0] == tm
    npb = n_p // tm
    t = n_p + tm
    wa = w_o[:POOL_W].astype(BF16)
    wb = w_o[POOL_W:].astype(BF16)
    prow = lambda w: pl.BlockSpec((tm, w), lambda i: (jnp.minimum(i, npb - 1), 0))
    srow = lambda w: pl.BlockSpec((tm, w), lambda i: (0, 0))
    full = lambda a: pl.BlockSpec(a.shape, lambda i: (0,) * a.ndim)
    return pl.pallas_call(
        functools.partial(_outproj_kernel, n_prompt_blocks=npb),
        grid=(npb + 1,),
        in_specs=[prow(d), prow(POOL_W), prow(ybp.shape[1]), srow(d), srow(POOL_W), srow(ybs.shape[1]),
                  full(wa), full(wb), full(g2)],
        out_specs=[pl.BlockSpec((tm, d), lambda i: (i, 0)), pl.BlockSpec((d, tm), lambda i: (0, i))],
        out_shape=[jax.ShapeDtypeStruct((t, d), F32), jax.ShapeDtypeStruct((d, t), BF16)],
        compiler_params=_cparams(("arbitrary",)),
        name="outproj",
    )(xp, yap, ybp, xs, yas, ybs, wa, wb, g2)


DSA_TQ = 256
INT_MIN = -2 ** 31


def _order_key(score):
    b = pltpu.bitcast(score + 0.0, I32)
    return jnp.where(b < 0, b ^ 0x7FFFFFFF, b)


def _kth_largest_key(count_ge, k, shape):
    def bit_step(i, t_u):
        cand_u = t_u | (jnp.int32(1) << (31 - i))
        cnt = count_ge(cand_u ^ INT_MIN)
        return jnp.where(cnt >= k, cand_u, t_u)
    t_u = lax.fori_loop(0, 32, bit_step, jnp.zeros(shape, I32))
    return t_u ^ INT_MIN


def _dsa_prompt_kernel(iq_ref, iwt_ref, q_ref, ik_ref, k_ref, vt_ref, o_ref,
                       key_scr, bias_scr, tie_scr, carry_scr, *, k_sel):
    tq = DSA_TQ
    qb = pl.program_id(1)
    n_chunks = qb + 1
    krow = lax.broadcasted_iota(I32, (tq, tq), 0)
    qcol = lax.broadcasted_iota(I32, (tq, tq), 1)
    nt = (((1,), (1,)), ((), ()))

    def chunk_start(j):
        return pl.multiple_of(j * tq, tq)

    def score_chunk(j, c):
        ikc = ik_ref[0, pl.ds(chunk_start(j), tq), :]
        sc = jnp.zeros((tq, tq), F32)
        for h in range(IDX_HEADS):
            s = lax.dot_general(ikc, iq_ref[0, :, h * IDX_DIM:(h + 1) * IDX_DIM], nt,
                                preferred_element_type=F32)
            sc = sc + jnp.maximum(s, 0.0) * iwt_ref[0, h:h + 1, :]
        key = _order_key(sc)
        key = jnp.where((j < qb) | (krow <= qcol), key, INT_MIN)
        key_scr[pl.ds(chunk_start(j), tq), :] = key
        return c

    lax.fori_loop(0, n_chunks, score_chunk, 0)

    def count(pred):
        def body(j, acc):
            kc = key_scr[pl.ds(chunk_start(j), tq), :]
            m = jnp.where(pred(kc), 1, 0)
            return acc + jnp.sum(m.reshape(tq // SUBLANES, SUBLANES, tq), axis=0)
        acc = lax.fori_loop(0, n_chunks, body, jnp.zeros((SUBLANES, tq), I32))
        return jnp.sum(acc, axis=0, keepdims=True)

    thr = _kth_largest_key(lambda t: count(lambda kc: kc >= t), k_sel, (1, tq))
    n_gt = count(lambda kc: kc > thr)
    n_eq = count(lambda kc: (kc == thr) & (kc > INT_MIN))
    room = k_sel - n_gt
    ties_cut = jnp.max(n_eq - room) > 0

    carry_scr[...] = jnp.zeros(carry_scr.shape, F32)
    scale = HEAD_DIM ** -0.5

    def logits(j, h, bias):
        kch = k_ref[0, pl.ds(chunk_start(j), tq), :]
        qh = q_ref[0, :, h * HEAD_DIM:(h + 1) * HEAD_DIM]
        return lax.dot_general(kch, qh, nt, preferred_element_type=F32) * scale + bias

    def max_chunk(j, ms):
        kc = key_scr[pl.ds(chunk_start(j), tq), :]
        eq = (kc == thr) & (kc > INT_MIN)

        @pl.when(ties_cut)
        def _():
            tri = jnp.where(qcol <= krow, 1.0, 0.0).astype(BF16)
            eqf = jnp.where(eq, 1.0, 0.0)
            prefix = jnp.dot(tri, eqf.astype(BF16), preferred_element_type=F32) + carry_scr[...]
            tie_scr[...] = jnp.where(prefix <= room.astype(F32), 1.0, 0.0)
            carry_scr[...] = carry_scr[...] + jnp.sum(eqf, axis=0, keepdims=True)

        @pl.when(jnp.logical_not(ties_cut))
        def _():
            tie_scr[...] = jnp.ones(tie_scr.shape, F32)

        sel = (kc > thr) | (eq & (tie_scr[...] > 0.0))
        bias = jnp.where(sel, 0.0, NEG_INF)
        bias_scr[pl.ds(chunk_start(j), tq), :] = bias
        return tuple(jnp.maximum(ms[h], jnp.max(logits(j, h, bias), axis=0, keepdims=True))
                     for h in range(N_HEADS))

    ms = lax.fori_loop(0, n_chunks, max_chunk,
                       tuple(jnp.full((1, tq), NEG_INF, F32) for _ in range(N_HEADS)))

    def attend_chunk(j, carry):
        dens, nums = carry
        bias = bias_scr[pl.ds(chunk_start(j), tq), :]
        vtc = vt_ref[0, :, pl.ds(chunk_start(j), tq)]
        ps = [jnp.exp(logits(j, h, bias) - ms[h]) for h in range(N_HEADS)]
        dens = tuple(dens[h] + jnp.sum(ps[h], axis=0, keepdims=True) for h in range(N_HEADS))
        nums = tuple(nums[h] + jnp.dot(vtc, ps[h].astype(BF16), preferred_element_type=F32)
                     for h in range(N_HEADS))
        return dens, nums

    dens, nums = lax.fori_loop(
        0, n_chunks, attend_chunk,
        (tuple(jnp.zeros((1, tq), F32) for _ in range(N_HEADS)),
         tuple(jnp.zeros((HEAD_DIM, tq), F32) for _ in range(N_HEADS))))
    for h in range(N_HEADS):
        o_ref[0, :, h * HEAD_DIM:(h + 1) * HEAD_DIM] = (nums[h] / dens[h]).T.astype(o_ref.dtype)


def _dsa_prompt_pallas(q, k, vt, iq, ik, iwt):
    b, s, _ = q.shape
    tq = DSA_TQ
    assert s % tq == 0
    k_sel = min(TOPK_MAX, s // 4)
    blk = lambda w: pl.BlockSpec((1, tq, w), lambda bi, qi: (bi, qi, 0))
    seq = lambda w: pl.BlockSpec((1, s, w), lambda bi, qi: (bi, 0, 0))
    return pl.pallas_call(
        functools.partial(_dsa_prompt_kernel, k_sel=k_sel),
        grid=(b, s // tq),
        in_specs=[blk(IDX_HEADS * IDX_DIM),
                  pl.BlockSpec((1, IDX_HEADS, tq), lambda bi, qi: (bi, 0, qi)),
                  blk(N_HEADS * HEAD_DIM), seq(IDX_DIM), seq(HEAD_DIM),
                  pl.BlockSpec((1, HEAD_DIM, s), lambda bi, qi: (bi, 0, 0))],
        out_specs=blk(N_HEADS * HEAD_DIM),
        out_shape=jax.ShapeDtypeStruct((b, s, N_HEADS * HEAD_DIM), BF16),
        scratch_shapes=[
            pltpu.VMEM((s, tq), I32),
            pltpu.VMEM((s, tq), F32),
            pltpu.VMEM((tq, tq), F32),
            pltpu.VMEM((1, tq), F32),
        ],
        compiler_params=_cparams(("arbitrary", "arbitrary")),
        name="dsa_prompt",
    )(iq, iwt, q, ik, k, vt)


DEC_ROWS = 8
NEW_PAD = 128


def _dsa_sample_kernel(pt_ref, iq_ref, wsel_ref, q_ref, ikn_ref, kn_ref, vn_ref, cik_ref, ck_ref, cv_ref,
                       o_ref, ikbuf, kbuf, vbuf, sem, tie_scr, *, n_pages, n_new, k_sel):
    bi = pl.program_id(0)
    nb = pl.num_programs(0)
    slot = bi % 2
    n_past = n_pages * PAGE_SIZE
    nt = (((1,), (1,)), ((), ()))

    def page_copies(seq, sl):
        cps = []
        for p in range(n_pages):
            pg = pt_ref[seq, p]
            cps.append(pltpu.make_async_copy(cik_ref.at[pg], ikbuf.at[sl, p], sem.at[sl, 0]))
            cps.append(pltpu.make_async_copy(ck_ref.at[pg], kbuf.at[sl, p], sem.at[sl, 1]))
            cps.append(pltpu.make_async_copy(cv_ref.at[pg], vbuf.at[sl, p], sem.at[sl, 2]))
        return cps

    @pl.when(bi == 0)
    def _():
        for cp in page_copies(0, 0):
            cp.start()

    @pl.when(bi + 1 < nb)
    def _():
        for cp in page_copies(bi + 1, 1 - slot):
            cp.start()

    for cp in page_copies(bi, slot):
        cp.wait()

    iq = iq_ref[...]
    wsel = wsel_ref[...]
    ik_all = ikbuf[slot].reshape(n_past, IDX_DIM).astype(BF16)
    s_past = lax.dot_general(iq, ik_all, nt, preferred_element_type=F32)
    s_new = lax.dot_general(iq, ikn_ref[...], nt, preferred_element_type=F32)
    hi = lax.Precision.HIGHEST
    sc_past = jnp.dot(wsel, jnp.maximum(s_past, 0.0), precision=hi, preferred_element_type=F32)
    sc_new = jnp.dot(wsel, jnp.maximum(s_new, 0.0), precision=hi, preferred_element_type=F32)
    row_p = lax.broadcasted_iota(I32, (DEC_ROWS, n_past), 0)
    row_n = lax.broadcasted_iota(I32, (DEC_ROWS, NEW_PAD), 0)
    col_n = lax.broadcasted_iota(I32, (DEC_ROWS, NEW_PAD), 1)
    key_p = jnp.where(row_p < n_new, _order_key(sc_past), INT_MIN)
    key_n = jnp.where((row_n < n_new) & (col_n <= row_n), _order_key(sc_new), INT_MIN)

    def count(pred):
        one = lambda m: jnp.sum(jnp.where(m, 1, 0), axis=1, keepdims=True)
        return one(pred(key_p)) + one(pred(key_n))

    thr = _kth_largest_key(lambda t: count(lambda kc: kc >= t), k_sel, (DEC_ROWS, 1))
    n_gt = count(lambda kc: kc > thr)
    eq_p = (key_p == thr) & (key_p > INT_MIN)
    eq_n = (key_n == thr) & (key_n > INT_MIN)
    n_eq = (jnp.sum(jnp.where(eq_p, 1, 0), axis=1, keepdims=True)
            + jnp.sum(jnp.where(eq_n, 1, 0), axis=1, keepdims=True))
    room = k_sel - n_gt
    ties_cut = jnp.max(n_eq - room) > 0

    @pl.when(ties_cut)
    def _():
        r = lax.broadcasted_iota(I32, (LANES, LANES), 0)
        c = lax.broadcasted_iota(I32, (LANES, LANES), 1)
        tri = jnp.where(r <= c, 1.0, 0.0).astype(BF16)
        roomf = room.astype(F32)
        carry = jnp.zeros((DEC_ROWS, 1), F32)
        eqf_p = jnp.where(eq_p, 1.0, 0.0)
        for ch in range(n_past // LANES):
            e = eqf_p[:, ch * LANES:(ch + 1) * LANES]
            prefix = jnp.dot(e.astype(BF16), tri, preferred_element_type=F32) + carry
            tie_scr[:, ch * LANES:(ch + 1) * LANES] = jnp.where(prefix <= roomf, 1.0, 0.0)
            carry = carry + jnp.sum(e, axis=1, keepdims=True)
        e = jnp.where(eq_n, 1.0, 0.0)
        prefix = jnp.dot(e.astype(BF16), tri, preferred_element_type=F32) + carry
        tie_scr[:, n_past:n_past + NEW_PAD] = jnp.where(prefix <= roomf, 1.0, 0.0)

    @pl.when(jnp.logical_not(ties_cut))
    def _():
        tie_scr[...] = jnp.ones(tie_scr.shape, F32)

    sel_p = jnp.where((key_p > thr) | (eq_p & (tie_scr[:, 0:n_past] > 0.0)), 1.0, 0.0)
    sel_n = jnp.where((key_n > thr) | (eq_n & (tie_scr[:, n_past:n_past + NEW_PAD] > 0.0)), 1.0, 0.0)
    rr = lax.broadcasted_iota(I32, (N_HEADS * n_new, DEC_ROWS), 0)
    rc = lax.broadcasted_iota(I32, (N_HEADS * n_new, DEC_ROWS), 1)
    hit = rr == rc
    for hh in range(1, N_HEADS):
        hit = hit | (rr == rc + hh * n_new)
    rep = jnp.where(hit & (rc < n_new), 1.0, 0.0).astype(BF16)
    m_p = jnp.dot(rep, sel_p.astype(BF16), preferred_element_type=F32)
    m_n = jnp.dot(rep, sel_n.astype(BF16), preferred_element_type=F32)

    q = q_ref[...]
    scale = HEAD_DIM ** -0.5
    k_all = kbuf[slot].reshape(n_past, HEAD_DIM).astype(BF16)
    lg_p = jnp.where(m_p > 0.5, lax.dot_general(q, k_all, nt, preferred_element_type=F32) * scale, NEG_INF)
    lg_n = jnp.where(m_n > 0.5, lax.dot_general(q, kn_ref[...], nt, preferred_element_type=F32) * scale,
                     NEG_INF)
    mx = jnp.maximum(jnp.max(lg_p, axis=1, keepdims=True), jnp.max(lg_n, axis=1, keepdims=True))
    p_p = jnp.exp(lg_p - mx)
    p_n = jnp.exp(lg_n - mx)
    den = jnp.sum(p_p, axis=1, keepdims=True) + jnp.sum(p_n, axis=1, keepdims=True)
    v_all = vbuf[slot].reshape(n_past, HEAD_DIM).astype(BF16)
    num = (jnp.dot(p_p.astype(BF16), v_all, preferred_element_type=F32)
           + jnp.dot(p_n.astype(BF16), vn_ref[...], preferred_element_type=F32))
    o_ref[...] = num / den


def _dsa_sample_pallas(q, kb, vb, iq, ikb, iw, cache_k, cache_v, cache_ik, page_table):
    db, ds, _ = q.shape
    n_pages = page_table.shape[1]
    n_past = n_pages * PAGE_SIZE
    assert ds <= DEC_ROWS and ds <= NEW_PAD
    k_sel = min(TOPK_MAX, (n_past + ds) // 4)
    heads_first = lambda a, nh: a.reshape(db, ds, nh, -1).transpose(0, 2, 1, 3).reshape(db, nh * ds, -1)
    iq_r = heads_first(iq, IDX_HEADS)
    q_r = heads_first(q, N_HEADS)
    wsel = jnp.einsum('bjh,jk->bjhk', iw, jnp.eye(ds, dtype=F32)).reshape(db, ds, IDX_HEADS * ds)
    wsel = jnp.pad(wsel, ((0, 0), (0, DEC_ROWS - ds), (0, 0)))
    pad_new = lambda a: jnp.pad(a, ((0, 0), (0, NEW_PAD - ds), (0, 0)))
    per_seq = lambda a: pl.BlockSpec((None,) + a.shape[1:], lambda i, pt: (i, 0, 0))
    any_spec = pl.BlockSpec(memory_space=pl.ANY)
    ins = (iq_r, wsel, q_r, pad_new(ikb), pad_new(kb), pad_new(vb))
    out = pl.pallas_call(
        functools.partial(_dsa_sample_kernel, n_pages=n_pages, n_new=ds, k_sel=k_sel),
        grid_spec=pltpu.PrefetchScalarGridSpec(
            num_scalar_prefetch=1,
            grid=(db,),
            in_specs=[per_seq(a) for a in ins] + [any_spec, any_spec, any_spec],
            out_specs=pl.BlockSpec((None, N_HEADS * ds, HEAD_DIM), lambda i, pt: (i, 0, 0)),
            scratch_shapes=[
                pltpu.VMEM((2, n_pages, PAGE_SIZE, IDX_DIM), F32),
                pltpu.VMEM((2, n_pages, PAGE_SIZE, HEAD_DIM), F32),
                pltpu.VMEM((2, n_pages, PAGE_SIZE, HEAD_DIM), F32),
                pltpu.SemaphoreType.DMA((2, 3)),
                pltpu.VMEM((DEC_ROWS, n_past + NEW_PAD), F32),
            ],
        ),
        out_shape=jax.ShapeDtypeStruct((db, N_HEADS * ds, HEAD_DIM), F32),
        compiler_params=_cparams(("arbitrary",)),
        name="dsa_sample",
    )(page_table, *ins, cache_ik, cache_k, cache_v)
    return out.reshape(db, N_HEADS, ds, HEAD_DIM).transpose(0, 2, 1, 3).reshape(db, ds, N_HEADS * HEAD_DIM)


def kernel(x_prompt, x_sample, cache_k, cache_v, cache_ik, state_pool, page_table, norm1_g, w_in,
           q_norm_g, k_norm_g, w_pool, pool_scale, w_o, norm2_g, w_pq, sub_keys, expert_u, expert_v):
    b, s, d = x_prompt.shape
    db, ds, _ = x_sample.shape
    past = page_table.shape[1] * PAGE_SIZE
    w_p = _pack_w_in(w_in[0])
    row = lambda a: a[0][None, :]
    shared = (row(norm1_g), w_p, row(q_norm_g), row(k_norm_g), w_pool[0].astype(BF16), row(pool_scale))

    (ya, q, k1, v1, kb, vbt, iq, ik1, ikb, iwt, pool_p) = _inproj_prompt(
        x_prompt, _rope_tables(jnp.arange(s)), *shared)
    yb = _dsa_prompt_pallas(q, kb, vbt, iq, ikb, iwt)

    x_tm = x_sample.transpose(1, 0, 2).reshape(ds * db, d)
    tab_s = _rope_tables(past + jnp.repeat(jnp.arange(ds), db))
    (ya_s, q_s, k2, v2, kb_s, vb_s, iq_s, ik2, ikb_s, iw_s, pool_s) = _inproj_sample(
        x_tm, tab_s, *shared, state_pool[0].transpose(1, 0, 2), db, ds, past)
    bm = lambda a: a.reshape(ds, db, -1).transpose(1, 0, 2)
    k2, v2, ik2 = bm(k2), bm(v2), bm(ik2)
    yb_s = _dsa_sample_pallas(bm(q_s), bm(kb_s), bm(vb_s), bm(iq_s), bm(ikb_s), bm(iw_s),
                              cache_k[0], cache_v[0], cache_ik[0], page_table).astype(BF16)

    n_p, n_s = b * s, db * ds
    assert n_s <= OUT_TM and OUT_TM == PEER_TM
    flat = lambda a: a.reshape(n_p, -1)
    blk = lambda a: jnp.pad(a.reshape(n_s, -1), ((0, OUT_TM - n_s), (0, 0)))
    x2, h2t = _outproj(flat(x_prompt), flat(ya), flat(yb), blk(x_sample), blk(bm(ya_s)), blk(yb_s),
                       w_o[0], row(norm2_g))
    y_p, y_s = _peer(x2, h2t, w_pq[0], sub_keys[0], expert_u[0], expert_v[0])
    st = lambda a: a[None]
    return (y_p.reshape(b, s, d), y_s[:n_s].reshape(db, ds, d),
            st(k1), st(v1), st(ik1), st(pool_p[:, 1:]),
            st(k2), st(v2), st(ik2), st(pool_s.transpose(1, 0, 2)))
```

```python
import functools
import math

import jax
import jax.numpy as jnp
import numpy as np
from jax import lax
from jax.experimental import pallas as pl
from jax.experimental.pallas import tpu as pltpu

F32 = jnp.float32
BF16 = jnp.bfloat16
I32 = jnp.int32

LANES = 128
SUBLANES = 8
VMEM_LIMIT_BYTES = 56 * 1024 * 1024

D_MODEL = 1024
POOL_W = 512
POOL_WINDOWS = (2, 4, 8, 16)
POOL_GW = 128
POOL_STATE = 15
HEAD_DIM = 128
N_HEADS = 4
IDX_HEADS = 8
IDX_DIM = 64
TOPK_MAX = 256
ROPE_THETA = 10000.0
PAGE_SIZE = 128
N_KEYS = 128
PEER_HEADS = 8
PEER_TOPK = 16
PEER_HALF = 128
EPS = 1e-6
NEG_INF = float("-inf")


def _cparams(sem):
    return pltpu.CompilerParams(dimension_semantics=sem, vmem_limit_bytes=VMEM_LIMIT_BYTES)


PEER_TT = 512
CAND_ROWS = 16 + 8 * 15


def _top16_rows(x, iota0):
    vals, idxs = [], []
    for _ in range(PEER_TOPK):
        m = jnp.max(x, axis=0, keepdims=True)
        idx = jnp.min(jnp.where(x == m, iota0, N_KEYS), axis=0, keepdims=True)
        x = jnp.where(iota0 == idx, NEG_INF, x)
        vals.append(m)
        idxs.append(idx)
    return vals, idxs


def _peer_select_kernel(h2t_ref, wpqt_ref, sk_ref, n1_ref, w1_ref, r2_ref, p2_ref, s_scr):
    tt = h2t_ref.shape[1]
    qt = jnp.dot(wpqt_ref[...], h2t_ref[...], preferred_element_type=F32).astype(BF16)
    for hc in range(2 * PEER_HEADS):
        s_scr[hc] = jnp.dot(sk_ref[hc], qt[hc * PEER_HALF:(hc + 1) * PEER_HALF, :],
                            preferred_element_type=F32)

    iota0 = lax.broadcasted_iota(I32, (N_KEYS, tt), 0)
    rows = lax.broadcasted_iota(I32, (CAND_ROWS, tt), 0)
    ca = jnp.where(rows < 16, 0, ((rows - 16) >> 3) + 1)
    cb = jnp.where(rows < 16, rows, (rows - 16) & 7)
    cvalid = (ca + 1) * (cb + 1) <= PEER_TOPK
    cflat = ca * PEER_TOPK + cb

    n_cand = sum(1 for a in range(16) for b in range(16) if (a + 1) * (b + 1) <= PEER_TOPK)

    def head_fast(h):
        x1o = s_scr[2 * h]
        x2o = s_scr[2 * h + 1]
        sv1, sv2 = [], []
        x1 = x1o
        for _ in range(PEER_TOPK):
            m = jnp.max(x1, axis=0, keepdims=True)
            x1 = jnp.where(x1 == m, NEG_INF, x1)
            sv1.append(m)
        x2 = x2o
        r2 = jnp.full((N_KEYS, tt), float(N_KEYS - 1), F32)
        for r in range(PEER_TOPK):
            m = jnp.max(x2, axis=0, keepdims=True)
            hit = x2 == m
            x2 = jnp.where(hit, NEG_INF, x2)
            r2 = jnp.where(hit, float(r), r2)
            sv2.append(m)
        gone = lambda x: jnp.sum(jnp.where(x == NEG_INF, 1.0, 0.0), axis=0, keepdims=True)
        bad = (gone(x1) != PEER_TOPK) | (gone(x2) != PEER_TOPK)
        sv2a = jnp.concatenate(sv2, axis=0)
        cand = jnp.concatenate([sv1[0] + sv2a] + [sv1[a] + sv2a[:8] for a in range(1, 16)], axis=0)
        cand = jnp.where(cvalid, cand, NEG_INF)
        fv = []
        for _ in range(PEER_TOPK):
            m = jnp.max(cand, axis=0, keepdims=True)
            cand = jnp.where(cand == m, NEG_INF, cand)
            fv.append(m)
        bad = bad | (gone(cand) != PEER_TOPK + CAND_ROWS - n_cand)
        tau = fv[PEER_TOPK - 1]
        z = jnp.sum(jnp.exp(jnp.concatenate(fv, axis=0) - fv[0]), axis=0, keepdims=True)
        n1 = jnp.zeros((N_KEYS, tt), F32)
        for b in range(8):
            n1 = n1 + jnp.where(x1o + sv2[b] >= tau, 1.0, 0.0)
        n_top = jnp.sum(jnp.where(sv1[0] + sv2a >= tau, 1.0, 0.0), axis=0, keepdims=True)
        n1 = n1 + jnp.where(x1o == sv1[0], jnp.maximum(n_top - 8.0, 0.0), 0.0)
        bad = bad | (jnp.sum(n1, axis=0, keepdims=True) != PEER_TOPK)
        n1_ref[h] = n1
        w1_ref[h] = jnp.exp(x1o - sv1[0]) * (1.0 / z)
        r2_ref[h] = r2.astype(BF16)
        p2_ref[h] = jnp.exp(x2o - sv2[0]).astype(BF16)
        return bad

    def head(h, carry):
        bad = head_fast(h)

        @pl.when(jnp.max(jnp.where(bad, 1, 0)) > 0)
        def _():
            head_exact(h)

        return carry

    def head_exact(h):
        sv1, si1 = _top16_rows(s_scr[2 * h], iota0)
        x2 = s_scr[2 * h + 1]
        r2 = jnp.full((N_KEYS, tt), float(N_KEYS - 1), F32)
        p2 = jnp.zeros((N_KEYS, tt), F32)
        sv2 = []
        for r in range(PEER_TOPK):
            m = jnp.max(x2, axis=0, keepdims=True)
            idx = jnp.min(jnp.where(x2 == m, iota0, N_KEYS), axis=0, keepdims=True)
            hit = iota0 == idx
            x2 = jnp.where(hit, NEG_INF, x2)
            sv2.append(m)
            r2 = jnp.where(hit, float(r), r2)
            p2 = jnp.where(hit, jnp.exp(m - sv2[0]), p2)
        e1 = [jnp.exp(v - sv1[0]) for v in sv1]
        sv2a = jnp.concatenate(sv2, axis=0)
        e2a = jnp.exp(sv2a - sv2[0])
        cand = jnp.concatenate([sv1[0] + sv2a] + [sv1[a] + sv2a[:8] for a in range(1, 16)], axis=0)
        egrid = jnp.concatenate([e1[0] * e2a] + [e1[a] * e2a[:8] for a in range(1, 16)], axis=0)
        cand = jnp.where(cvalid, cand, NEG_INF)
        sel = jnp.zeros((CAND_ROWS, tt), F32)
        for _ in range(PEER_TOPK):
            m = jnp.max(cand, axis=0, keepdims=True)
            f = jnp.min(jnp.where(cand == m, cflat, PEER_TOPK * PEER_TOPK), axis=0, keepdims=True)
            hit = cflat == f
            cand = jnp.where(hit, NEG_INF, cand)
            sel = jnp.where(hit, 1.0, sel)
        z = jnp.sum(sel * egrid, axis=0, keepdims=True)
        inv_z = 1.0 / z
        n1 = jnp.zeros((N_KEYS, tt), F32)
        w1 = jnp.zeros((N_KEYS, tt), F32)
        for a in range(PEER_TOPK):
            lo = 0 if a == 0 else 16 + 8 * (a - 1)
            hi = 16 if a == 0 else lo + 8
            n_a = jnp.sum(sel[lo:hi], axis=0, keepdims=True)
            hit = iota0 == si1[a]
            n1 = jnp.where(hit, n_a, n1)
            w1 = jnp.where(hit, e1[a] * inv_z, w1)
        n1_ref[h] = n1
        w1_ref[h] = w1
        r2_ref[h] = r2.astype(BF16)
        p2_ref[h] = p2.astype(BF16)

    lax.fori_loop(0, PEER_HEADS, head, 0)


def _peer_select(h2t, wpqt, sk):
    t = h2t.shape[1]
    tt = PEER_TT
    assert t % tt == 0
    tok = lambda i: (0, 0, i)
    shp = (PEER_HEADS, N_KEYS, t)
    return pl.pallas_call(
        _peer_select_kernel,
        grid=(t // tt,),
        in_specs=[
            pl.BlockSpec((D_MODEL, tt), lambda i: (0, i)),
            pl.BlockSpec(wpqt.shape, lambda i: (0, 0)),
            pl.BlockSpec(sk.shape, lambda i: (0, 0, 0)),
        ],
        out_specs=[pl.BlockSpec((PEER_HEADS, N_KEYS, tt), tok)] * 4,
        out_shape=[jax.ShapeDtypeStruct(shp, F32), jax.ShapeDtypeStruct(shp, F32),
                   jax.ShapeDtypeStruct(shp, BF16), jax.ShapeDtypeStruct(shp, BF16)],
        scratch_shapes=[pltpu.VMEM((2 * PEER_HEADS, N_KEYS, tt), F32)],
        compiler_params=_cparams(("arbitrary",)),
        name="peer_select",
    )(h2t, wpqt, sk)


PEER_TM = 512
PEER_TE = 2048


def _gelu_exact(x):
    return 0.5 * x * (1.0 + lax.erf(x * (1.0 / math.sqrt(2.0))))


def _peer_expert_kernel(xt_ref, u_ref, vt_ref, n1_ref, w1_ref, r2_ref, p2_ref, x2_ref, out_ref, outs_ref,
                        acc_ref, m_ref):
    j = pl.program_id(1)
    nc = PEER_TE // N_KEYS
    tm = xt_ref.shape[1]

    @pl.when(j == 0)
    def _():
        acc_ref[...] = jnp.zeros_like(acc_ref)

    s = jnp.dot(u_ref[...], xt_ref[...], preferred_element_type=F32)
    c0 = pl.multiple_of(j * nc, nc)
    n_rows = [n1_ref[h, pl.ds(c0, nc), :].astype(BF16) for h in range(PEER_HEADS)]
    w_rows = [w1_ref[h, pl.ds(c0, nc), :].astype(BF16) for h in range(PEER_HEADS)]
    for cc in range(nc):
        a = _gelu_exact(s[cc * N_KEYS:(cc + 1) * N_KEYS, :])
        g = jnp.zeros((N_KEYS, tm), BF16)
        for h in range(PEER_HEADS):
            nb = jnp.broadcast_to(n_rows[h][cc:cc + 1, :], (N_KEYS, tm))
            wb = jnp.broadcast_to(w_rows[h][cc:cc + 1, :], (N_KEYS, tm))
            g = g + jnp.where(r2_ref[h] < nb, p2_ref[h], jnp.zeros((), BF16)) * wb
        m_ref[cc * N_KEYS:(cc + 1) * N_KEYS, :] = a.astype(BF16) * g
    acc_ref[...] += jnp.dot(vt_ref[...], m_ref[...], preferred_element_type=F32)

    last = j == pl.num_programs(1) - 1
    tail = pl.program_id(0) == pl.num_programs(0) - 1

    @pl.when(last & jnp.logical_not(tail))
    def _():
        out_ref[...] = x2_ref[...] + acc_ref[...].T

    @pl.when(last & tail)
    def _():
        outs_ref[...] = x2_ref[...] + acc_ref[...].T


def _peer_experts(xt, u, vt, n1, w1, r2, p2, x2):
    d, t = xt.shape
    e = u.shape[0]
    tm, te = PEER_TM, PEER_TE
    assert t % tm == 0 and e % te == 0
    nblk = e // te
    ntb = t // tm
    fac = pl.BlockSpec((PEER_HEADS, N_KEYS, tm), lambda i, j: (0, 0, i))
    return pl.pallas_call(
        _peer_expert_kernel,
        grid=(ntb, nblk),
        in_specs=[
            pl.BlockSpec((d, tm), lambda i, j: (0, i)),
            pl.BlockSpec((te, d), lambda i, j: (j, 0)),
            pl.BlockSpec((None, d, te), lambda i, j: (j, 0, 0)),
            fac, fac, fac, fac,
            pl.BlockSpec((tm, d), lambda i, j: (i, 0)),
        ],
        out_specs=[pl.BlockSpec((tm, d), lambda i, j: (jnp.minimum(i, ntb - 2), 0)),
                   pl.BlockSpec((tm, d), lambda i, j: (0, 0))],
        out_shape=[jax.ShapeDtypeStruct((t - tm, d), F32), jax.ShapeDtypeStruct((tm, d), F32)],
        scratch_shapes=[pltpu.VMEM((d, tm), F32),
                        pltpu.VMEM((te, tm), BF16)],
        compiler_params=_cparams(("arbitrary", "arbitrary")),
        name="peer_experts",
    )(xt, u, vt, n1, w1, r2, p2, x2)


def _peer(x2, h2t, w_pq, sub_keys, expert_u, expert_v):
    wpqt = w_pq.T.astype(BF16)
    sk = sub_keys.reshape(2 * PEER_HEADS, N_KEYS, PEER_HALF).astype(BF16)
    n1, w1, r2, p2 = _peer_select(h2t, wpqt, sk)
    vt = expert_v.astype(BF16).reshape(-1, PEER_TE, expert_v.shape[1]).transpose(0, 2, 1)
    return _peer_experts(h2t, expert_u.astype(BF16), vt, n1, w1, r2, p2, x2)


IN_TM = 256
C_U, C_Q, C_K, C_V, C_IQ, C_IK, C_IW, C_END = 0, 512, 1024, 1152, 1280, 1792, 1920, 2048
POOL_HALO = 16


def _pack_w_in(w_in):
    o = np.cumsum((0, POOL_W, N_HEADS * HEAD_DIM, HEAD_DIM, HEAD_DIM, IDX_HEADS * IDX_DIM, IDX_DIM, IDX_HEADS))
    z = lambda n: jnp.zeros((D_MODEL, n), w_in.dtype)
    return jnp.concatenate([w_in[:, o[0]:o[5]], w_in[:, o[5]:o[6]], z(C_IW - C_IK - IDX_DIM),
                            w_in[:, o[6]:o[7]], z(C_END - C_IW - IDX_HEADS)], axis=1).astype(BF16)


def _rope_tables(pos):
    def tab(half, reps):
        inv = ROPE_THETA ** (-jnp.arange(half, dtype=F32) / half)
        ang = pos.astype(F32)[:, None] * inv[None, :]
        c, s = jnp.cos(ang), jnp.sin(ang)
        return jnp.tile(jnp.concatenate([c, c], 1), (1, reps)), jnp.tile(jnp.concatenate([-s, s], 1), (1, reps))
    c128, s128 = tab(HEAD_DIM // 2, 1)
    c64, s64 = tab(IDX_DIM // 2, 2)
    return jnp.concatenate([c128, s128, c64, s64], axis=1)


def _rope128(x, tab):
    return x * tab[:, 0:128] + pltpu.roll(x, 64, 1) * tab[:, 128:256]


def _rope64x2(x, tab):
    lane = lax.broadcasted_iota(I32, x.shape, 1)
    partner = jnp.where((lane & 63) < 32, pltpu.roll(x, 96, 1), pltpu.roll(x, 32, 1))
    return x * tab[:, 256:384] + partner * tab[:, 384:512]


def _head_norm(x, g):
    return x * lax.rsqrt(jnp.mean(x * x, axis=-1, keepdims=True) + EPS) * g


def _project_block(x_ref, tab_ref, g1_ref, w_ref, qg_ref, kg_ref,
                   q_ref, k_ref, v_ref, kb_ref, vb_ref, iq_ref, ik_ref, ikb_ref, iw_ref, keys_major=False):
    x = x_ref[...]
    h = x * lax.rsqrt(jnp.mean(x * x, axis=-1, keepdims=True) + EPS) * g1_ref[...]
    z = jnp.dot(h.astype(BF16), w_ref[...], preferred_element_type=F32)
    tab = tab_ref[...]
    for hh in range(N_HEADS):
        sl = slice(C_Q + hh * HEAD_DIM, C_Q + (hh + 1) * HEAD_DIM)
        q_ref[:, hh * HEAD_DIM:(hh + 1) * HEAD_DIM] = _rope128(_head_norm(z[:, sl], qg_ref[...]), tab).astype(BF16)
    k = _rope128(_head_norm(z[:, C_K:C_V], kg_ref[...]), tab)
    v = z[:, C_V:C_IQ]
    k_ref[...] = k
    v_ref[...] = v
    kb_ref[...] = k.astype(BF16)
    vb_ref[...] = (v.T if keys_major else v).astype(BF16)
    for hp in range(IDX_HEADS // 2):
        sl = slice(C_IQ + hp * 128, C_IQ + (hp + 1) * 128)
        iq_ref[:, hp * 128:(hp + 1) * 128] = _rope64x2(z[:, sl], tab).astype(BF16)
    ik = _rope64x2(z[:, C_IK:C_IW], tab)[:, :IDX_DIM]
    ik_ref[...] = ik
    ikb_ref[...] = ik.astype(BF16)
    iw = z[:, C_IW:C_END] * (IDX_HEADS ** -0.5)
    iw_ref[...] = iw.T[:IDX_HEADS, :] if keys_major else iw[:, :IDX_HEADS]
    return z[:, C_U:C_Q]


def _pool_out(d_list, wp_ref, ps_ref, ya_ref, rows):
    for g in range(len(POOL_WINDOWS)):
        y = jnp.dot(d_list[g].astype(BF16), wp_ref[g], preferred_element_type=F32)
        ya_ref[rows, g * POOL_GW:(g + 1) * POOL_GW] = (y * ps_ref[:, g * POOL_GW:(g + 1) * POOL_GW]).astype(BF16)


def _inproj_prompt_kernel(x_ref, tab_ref, g1_ref, w_ref, qg_ref, kg_ref, wp_ref, ps_ref,
                          ya_ref, q_ref, k_ref, v_ref, kb_ref, vb_ref, iq_ref, ik_ref, ikb_ref, iw_ref,
                          pool_ref, ext_scr):
    tm = x_ref.shape[0]
    blk = pl.program_id(1)

    @pl.when(blk == 0)
    def _():
        ext_scr[0:POOL_HALO, :] = jnp.zeros((POOL_HALO, POOL_W), F32)

    u = _project_block(x_ref, tab_ref, g1_ref, w_ref, qg_ref, kg_ref,
                       q_ref, k_ref, v_ref, kb_ref, vb_ref, iq_ref, ik_ref, ikb_ref, iw_ref, keys_major=True)
    ext_scr[POOL_HALO:POOL_HALO + tm, :] = u
    pos = blk * tm + lax.broadcasted_iota(I32, (tm, 1), 0)
    d_list = []
    for g, w in enumerate(POOL_WINDOWS):
        sl = slice(g * POOL_GW, (g + 1) * POOL_GW)
        acc = u[:, sl]
        for j in range(1, w):
            acc = acc + ext_scr[POOL_HALO - j:POOL_HALO - j + tm, sl]
        cnt = jnp.minimum(pos + 1, w).astype(F32)
        d_list.append(acc / cnt - u[:, sl])
    _pool_out(d_list, wp_ref, ps_ref, ya_ref, slice(None))
    tail = ext_scr[tm:tm + POOL_HALO, :]
    pool_ref[...] = tail
    ext_scr[0:POOL_HALO, :] = tail


def _inproj_outs(n, lead):
    sd = lambda w, dt: jax.ShapeDtypeStruct(lead + (n, w), dt)
    return [sd(POOL_W, BF16), sd(N_HEADS * HEAD_DIM, BF16), sd(HEAD_DIM, F32), sd(HEAD_DIM, F32),
            sd(HEAD_DIM, BF16), sd(HEAD_DIM, BF16), sd(IDX_HEADS * IDX_DIM, BF16), sd(IDX_DIM, F32),
            sd(IDX_DIM, BF16), sd(IDX_HEADS, F32)]


def _inproj_prompt(x, tab, g1, w_p, qg, kg, wp, ps):
    b, s, d = x.shape
    tm = IN_TM
    assert s % tm == 0
    full = lambda a: pl.BlockSpec(a.shape, lambda bi, i: (0,) * a.ndim)
    rows = lambda w: pl.BlockSpec((None, tm, w), lambda bi, i: (bi, i, 0))
    cols = lambda h: pl.BlockSpec((None, h, tm), lambda bi, i: (bi, 0, i))
    outs = _inproj_outs(s, (b,)) + [jax.ShapeDtypeStruct((b, POOL_HALO, POOL_W), F32)]
    out_specs = [rows(o.shape[-1]) for o in outs[:-1]] + [
        pl.BlockSpec((None, POOL_HALO, POOL_W), lambda bi, i: (bi, 0, 0))]
    outs[5] = jax.ShapeDtypeStruct((b, HEAD_DIM, s), BF16)
    outs[9] = jax.ShapeDtypeStruct((b, IDX_HEADS, s), F32)
    out_specs[5], out_specs[9] = cols(HEAD_DIM), cols(IDX_HEADS)
    return pl.pallas_call(
        _inproj_prompt_kernel,
        grid=(b, s // tm),
        in_specs=[rows(d), pl.BlockSpec((tm, tab.shape[1]), lambda bi, i: (i, 0)),
                  full(g1), full(w_p), full(qg), full(kg), full(wp), full(ps)],
        out_specs=out_specs,
        out_shape=outs,
        scratch_shapes=[pltpu.VMEM((POOL_HALO + tm, POOL_W), F32)],
        compiler_params=_cparams(("arbitrary", "arbitrary")),
        name="inproj_prompt",
    )(x, tab, g1, w_p, qg, kg, wp, ps)


def _inproj_sample_kernel(x_ref, tab_ref, g1_ref, w_ref, qg_ref, kg_ref, wp_ref, ps_ref, st_ref,
                          ya_ref, q_ref, k_ref, v_ref, kb_ref, vb_ref, iq_ref, ik_ref, ikb_ref, iw_ref,
                          pool_ref, *, n_seq, n_new, past):
    u = _project_block(x_ref, tab_ref, g1_ref, w_ref, qg_ref, kg_ref,
                       q_ref, k_ref, v_ref, kb_ref, vb_ref, iq_ref, ik_ref, ikb_ref, iw_ref)
    hist = [st_ref[i] for i in range(POOL_STATE)] + [u[j * n_seq:(j + 1) * n_seq, :] for j in range(n_new)]
    for j in range(n_new):
        d_list = []
        for g, w in enumerate(POOL_WINDOWS):
            sl = slice(g * POOL_GW, (g + 1) * POOL_GW)
            acc = hist[POOL_STATE + j][:, sl]
            for i in range(1, w):
                acc = acc + hist[POOL_STATE + j - i][:, sl]
            d_list.append(acc / float(min(past + j + 1, w)) - hist[POOL_STATE + j][:, sl])
        _pool_out(d_list, wp_ref, ps_ref, ya_ref, slice(j * n_seq, (j + 1) * n_seq))
    for i in range(POOL_STATE):
        pool_ref[i] = hist[n_new + i]


def _inproj_sample(x_tm, tab, g1, w_p, qg, kg, wp, ps, st_tm, n_seq, n_new, past):
    n = x_tm.shape[0]
    outs = _inproj_outs(n, ()) + [jax.ShapeDtypeStruct((POOL_STATE, n_seq, POOL_W), F32)]
    return pl.pallas_call(
        functools.partial(_inproj_sample_kernel, n_seq=n_seq, n_new=n_new, past=past),
        out_shape=outs,
        compiler_params=pltpu.CompilerParams(vmem_limit_bytes=VMEM_LIMIT_BYTES),
        name="inproj_sample",
    )(x_tm, tab, g1, w_p, qg, kg, wp, ps, st_tm)


OUT_TM = 512


def _outproj_kernel(xp_ref, yap_ref, ybp_ref, xs_ref, yas_ref, ybs_ref, wa_ref, wb_ref, g2_ref,
                    x2_ref, h2t_ref, *, n_prompt_blocks):
    def rows(x_ref, ya_ref, yb_ref):
        x2 = (x_ref[...] + jnp.dot(ya_ref[...], wa_ref[...], preferred_element_type=F32)
              + jnp.dot(yb_ref[...], wb_ref[...], preferred_element_type=F32))
        x2_ref[...] = x2
        h2 = x2 * lax.rsqrt(jnp.mean(x2 * x2, axis=-1, keepdims=True) + EPS) * g2_ref[...]
        h2t_ref[...] = h2.T.astype(BF16)

    @pl.when(pl.program_id(0) < n_prompt_blocks)
    def _():
        rows(xp_ref, yap_ref, ybp_ref)

    @pl.when(pl.program_id(0) == n_prompt_blocks)
    def _():
        rows(xs_ref, yas_ref, ybs_ref)


def _outproj(xp, yap, ybp, xs, yas, ybs, w_o, g2):
    n_p, d = xp.shape
    tm = OUT_TM
    assert n_p % tm == 0 and xs.shape[0] == tm
    npb = n_p // tm
    t = n_p + tm
    wa = w_o[:POOL_W].astype(BF16)
    wb = w_o[POOL_W:].astype(BF16)
    prow = lambda w: pl.BlockSpec((tm, w), lambda i: (jnp.minimum(i, npb - 1), 0))
    srow = lambda w: pl.BlockSpec((tm, w), lambda i: (0, 0))
    full = lambda a: pl.BlockSpec(a.shape, lambda i: (0,) * a.ndim)
    return pl.pallas_call(
        functools.partial(_outproj_kernel, n_prompt_blocks=npb),
        grid=(npb + 1,),
        in_specs=[prow(d), prow(POOL_W), prow(ybp.shape[1]), srow(d), srow(POOL_W), srow(ybs.shape[1]),
                  full(wa), full(wb), full(g2)],
        out_specs=[pl.BlockSpec((tm, d), lambda i: (i, 0)), pl.BlockSpec((d, tm), lambda i: (0, i))],
        out_shape=[jax.ShapeDtypeStruct((t, d), F32), jax.ShapeDtypeStruct((d, t), BF16)],
        compiler_params=_cparams(("arbitrary",)),
        name="outproj",
    )(xp, yap, ybp, xs, yas, ybs, wa, wb, g2)


DSA_TQ = 256
PACK = 16
BF16_EXACT_INT = 256
COARSE_BITS = 14
COARSE_BIAS = 128
INT_MIN = -2 ** 31


def _order_key(score):
    b = pltpu.bitcast(score + 0.0, I32)
    return jnp.where(b < 0, b ^ 0x7FFFFFFF, b)


def _kth_largest_key(count_ge, k, shape, top=None):
    def bit_step(i, t_u):
        cand_u = t_u | (jnp.int32(1) << (first - i))
        cnt = count_ge(cand_u ^ INT_MIN)
        return jnp.where(cnt >= k, cand_u, t_u)
    first = 31 if top is None else 31 - COARSE_BITS
    t_u0 = jnp.zeros(shape, I32) if top is None else top << (32 - COARSE_BITS)
    t_u = lax.fori_loop(0, first + 1, bit_step, t_u0)
    return t_u ^ INT_MIN


def _coarse_bf16(prefix):
    return pltpu.bitcast((prefix + COARSE_BIAS) << 16, F32).astype(BF16)


def _dsa_prompt_kernel(iq_ref, iwt_ref, q_ref, ik_ref, k_ref, vt_ref, o_ref,
                       key_scr, hi_scr, bias_scr, tie_scr, carry_scr, *, k_sel):
    tq = DSA_TQ
    qb = pl.program_id(1)
    n_chunks = qb + 1
    krow = lax.broadcasted_iota(I32, (tq, tq), 0)
    qcol = lax.broadcasted_iota(I32, (tq, tq), 1)
    nt = (((1,), (1,)), ((), ()))

    def chunk_start(j):
        return pl.multiple_of(j * tq, tq)

    def score_chunk(j, c):
        ikc = ik_ref[0, pl.ds(chunk_start(j), tq), :]
        sc = jnp.zeros((tq, tq), F32)
        for h in range(IDX_HEADS):
            s = lax.dot_general(ikc, iq_ref[0, :, h * IDX_DIM:(h + 1) * IDX_DIM], nt,
                                preferred_element_type=F32)
            sc = sc + jnp.maximum(s, 0.0) * iwt_ref[0, h:h + 1, :]
        key = _order_key(sc)
        key = jnp.where((j < qb) | (krow <= qcol), key, INT_MIN)
        key_scr[pl.ds(chunk_start(j), tq), :] = key
        hi_scr[pl.ds(chunk_start(j), tq), :] = _coarse_bf16(
            lax.shift_right_logical(key ^ INT_MIN, 32 - COARSE_BITS))
        return c

    lax.fori_loop(0, n_chunks, score_chunk, 0)

    def count(pred):
        def body(j, acc):
            kc = key_scr[pl.ds(chunk_start(j), tq), :]
            m = jnp.where(pred(kc), 1, 0)
            return acc + jnp.sum(m.reshape(tq // SUBLANES, SUBLANES, tq), axis=0)
        acc = lax.fori_loop(0, n_chunks, body, jnp.zeros((SUBLANES, tq), I32))
        return jnp.sum(acc, axis=0, keepdims=True)

    def count_coarse(prefix):
        t = _coarse_bf16(prefix)
        def body(j, acc):
            hc = hi_scr[pl.ds(chunk_start(j), tq), :]
            m = jnp.where(hc >= t, jnp.ones((), BF16), jnp.zeros((), BF16))
            tiles = [m[i * PACK:(i + 1) * PACK, :] for i in range(tq // PACK)]
            while len(tiles) > 1:
                tiles = [a + b for a, b in zip(tiles[0::2], tiles[1::2])]
            return acc + tiles[0]
        acc = lax.fori_loop(0, n_chunks, body, jnp.zeros((PACK, tq), BF16))
        return jnp.sum(acc.astype(F32), axis=0, keepdims=True).astype(I32)

    def coarse_step(i, p):
        cand = p | (jnp.int32(1) << (COARSE_BITS - 1 - i))
        return jnp.where(count_coarse(cand) >= k_sel, cand, p)

    top = lax.fori_loop(0, COARSE_BITS, coarse_step, jnp.zeros((1, tq), I32))
    thr = _kth_largest_key(lambda t: count(lambda kc: kc >= t), k_sel, (1, tq), top=top)
    n_gt = count(lambda kc: kc > thr)
    n_eq = count(lambda kc: (kc == thr) & (kc > INT_MIN))
    room = k_sel - n_gt
    ties_cut = jnp.max(n_eq - room) > 0

    carry_scr[...] = jnp.zeros(carry_scr.shape, F32)
    scale = HEAD_DIM ** -0.5

    def logits(j, h, bias):
        kch = k_ref[0, pl.ds(chunk_start(j), tq), :]
        qh = q_ref[0, :, h * HEAD_DIM:(h + 1) * HEAD_DIM]
        return lax.dot_general(kch, qh, nt, preferred_element_type=F32) * scale + bias

    def max_chunk(j, ms):
        kc = key_scr[pl.ds(chunk_start(j), tq), :]
        eq = (kc == thr) & (kc > INT_MIN)

        @pl.when(ties_cut)
        def _():
            tri = jnp.where(qcol <= krow, 1.0, 0.0).astype(BF16)
            eqf = jnp.where(eq, 1.0, 0.0)
            prefix = jnp.dot(tri, eqf.astype(BF16), preferred_element_type=F32) + carry_scr[...]
            tie_scr[...] = jnp.where(prefix <= room.astype(F32), 1.0, 0.0)
            carry_scr[...] = carry_scr[...] + jnp.sum(eqf, axis=0, keepdims=True)

        @pl.when(jnp.logical_not(ties_cut))
        def _():
            tie_scr[...] = jnp.ones(tie_scr.shape, F32)

        sel = (kc > thr) | (eq & (tie_scr[...] > 0.0))
        bias = jnp.where(sel, 0.0, NEG_INF)
        bias_scr[pl.ds(chunk_start(j), tq), :] = bias
        return tuple(jnp.maximum(ms[h], jnp.max(logits(j, h, bias), axis=0, keepdims=True))
                     for h in range(N_HEADS))

    ms = lax.fori_loop(0, n_chunks, max_chunk,
                       tuple(jnp.full((1, tq), NEG_INF, F32) for _ in range(N_HEADS)))

    def attend_chunk(j, carry):
        dens, nums = carry
        bias = bias_scr[pl.ds(chunk_start(j), tq), :]
        vtc = vt_ref[0, :, pl.ds(chunk_start(j), tq)]
        ps = [jnp.exp(logits(j, h, bias) - ms[h]) for h in range(N_HEADS)]
        dens = tuple(dens[h] + jnp.sum(ps[h], axis=0, keepdims=True) for h in range(N_HEADS))
        nums = tuple(nums[h] + jnp.dot(vtc, ps[h].astype(BF16), preferred_element_type=F32)
                     for h in range(N_HEADS))
        return dens, nums

    dens, nums = lax.fori_loop(
        0, n_chunks, attend_chunk,
        (tuple(jnp.zeros((1, tq), F32) for _ in range(N_HEADS)),
         tuple(jnp.zeros((HEAD_DIM, tq), F32) for _ in range(N_HEADS))))
    for h in range(N_HEADS):
        o_ref[0, :, h * HEAD_DIM:(h + 1) * HEAD_DIM] = (nums[h] / dens[h]).T.astype(o_ref.dtype)


def _dsa_prompt_pallas(q, k, vt, iq, ik, iwt):
    b, s, _ = q.shape
    tq = DSA_TQ
    assert s % tq == 0
    assert (s // tq) * (tq // PACK) <= BF16_EXACT_INT
    k_sel = min(TOPK_MAX, s // 4)
    blk = lambda w: pl.BlockSpec((1, tq, w), lambda bi, qi: (bi, qi, 0))
    seq = lambda w: pl.BlockSpec((1, s, w), lambda bi, qi: (bi, 0, 0))
    return pl.pallas_call(
        functools.partial(_dsa_prompt_kernel, k_sel=k_sel),
        grid=(b, s // tq),
        in_specs=[blk(IDX_HEADS * IDX_DIM),
                  pl.BlockSpec((1, IDX_HEADS, tq), lambda bi, qi: (bi, 0, qi)),
                  blk(N_HEADS * HEAD_DIM), seq(IDX_DIM), seq(HEAD_DIM),
                  pl.BlockSpec((1, HEAD_DIM, s), lambda bi, qi: (bi, 0, 0))],
        out_specs=blk(N_HEADS * HEAD_DIM),
        out_shape=jax.ShapeDtypeStruct((b, s, N_HEADS * HEAD_DIM), BF16),
        scratch_shapes=[
            pltpu.VMEM((s, tq), I32),
            pltpu.VMEM((s, tq), BF16),
            pltpu.VMEM((s, tq), F32),
            pltpu.VMEM((tq, tq), F32),
            pltpu.VMEM((1, tq), F32),
        ],
        compiler_params=_cparams(("arbitrary", "arbitrary")),
        name="dsa_prompt",
    )(iq, iwt, q, ik, k, vt)


DEC_ROWS = 8
NEW_PAD = 128


def _dsa_sample_kernel(pt_ref, iq_ref, wsel_ref, q_ref, ikn_ref, kn_ref, vn_ref, cik_ref, ck_ref, cv_ref,
                       o_ref, ikbuf, kbuf, vbuf, sem, tie_scr, *, n_pages, n_new, k_sel):
    bi = pl.program_id(0)
    nb = pl.num_programs(0)
    slot = bi % 2
    n_past = n_pages * PAGE_SIZE
    nt = (((1,), (1,)), ((), ()))

    def page_copies(seq, sl):
        cps = []
        for p in range(n_pages):
            pg = pt_ref[seq, p]
            cps.append(pltpu.make_async_copy(cik_ref.at[pg], ikbuf.at[sl, p], sem.at[sl, 0]))
            cps.append(pltpu.make_async_copy(ck_ref.at[pg], kbuf.at[sl, p], sem.at[sl, 1]))
            cps.append(pltpu.make_async_copy(cv_ref.at[pg], vbuf.at[sl, p], sem.at[sl, 2]))
        return cps

    @pl.when(bi == 0)
    def _():
        for cp in page_copies(0, 0):
            cp.start()

    @pl.when(bi + 1 < nb)
    def _():
        for cp in page_copies(bi + 1, 1 - slot):
            cp.start()

    for cp in page_copies(bi, slot):
        cp.wait()

    iq = iq_ref[...]
    wsel = wsel_ref[...]
    ik_all = ikbuf[slot].reshape(n_past, IDX_DIM).astype(BF16)
    s_past = lax.dot_general(iq, ik_all, nt, preferred_element_type=F32)
    s_new = lax.dot_general(iq, ikn_ref[...], nt, preferred_element_type=F32)
    hi = lax.Precision.HIGHEST
    sc_past = jnp.dot(wsel, jnp.maximum(s_past, 0.0), precision=hi, preferred_element_type=F32)
    sc_new = jnp.dot(wsel, jnp.maximum(s_new, 0.0), precision=hi, preferred_element_type=F32)
    row_p = lax.broadcasted_iota(I32, (DEC_ROWS, n_past), 0)
    row_n = lax.broadcasted_iota(I32, (DEC_ROWS, NEW_PAD), 0)
    col_n = lax.broadcasted_iota(I32, (DEC_ROWS, NEW_PAD), 1)
    key_p = jnp.where(row_p < n_new, _order_key(sc_past), INT_MIN)
    key_n = jnp.where((row_n < n_new) & (col_n <= row_n), _order_key(sc_new), INT_MIN)

    def count(pred):
        one = lambda m: jnp.sum(jnp.where(m, 1, 0), axis=1, keepdims=True)
        return one(pred(key_p)) + one(pred(key_n))

    thr = _kth_largest_key(lambda t: count(lambda kc: kc >= t), k_sel, (DEC_ROWS, 1))
    n_gt = count(lambda kc: kc > thr)
    eq_p = (key_p == thr) & (key_p > INT_MIN)
    eq_n = (key_n == thr) & (key_n > INT_MIN)
    n_eq = (jnp.sum(jnp.where(eq_p, 1, 0), axis=1, keepdims=True)
            + jnp.sum(jnp.where(eq_n, 1, 0), axis=1, keepdims=True))
    room = k_sel - n_gt
    ties_cut = jnp.max(n_eq - room) > 0

    @pl.when(ties_cut)
    def _():
        r = lax.broadcasted_iota(I32, (LANES, LANES), 0)
        c = lax.broadcasted_iota(I32, (LANES, LANES), 1)
        tri = jnp.where(r <= c, 1.0, 0.0).astype(BF16)
        roomf = room.astype(F32)
        carry = jnp.zeros((DEC_ROWS, 1), F32)
        eqf_p = jnp.where(eq_p, 1.0, 0.0)
        for ch in range(n_past // LANES):
            e = eqf_p[:, ch * LANES:(ch + 1) * LANES]
            prefix = jnp.dot(e.astype(BF16), tri, preferred_element_type=F32) + carry
            tie_scr[:, ch * LANES:(ch + 1) * LANES] = jnp.where(prefix <= roomf, 1.0, 0.0)
            carry = carry + jnp.sum(e, axis=1, keepdims=True)
        e = jnp.where(eq_n, 1.0, 0.0)
        prefix = jnp.dot(e.astype(BF16), tri, preferred_element_type=F32) + carry
        tie_scr[:, n_past:n_past + NEW_PAD] = jnp.where(prefix <= roomf, 1.0, 0.0)

    @pl.when(jnp.logical_not(ties_cut))
    def _():
        tie_scr[...] = jnp.ones(tie_scr.shape, F32)

    sel_p = jnp.where((key_p > thr) | (eq_p & (tie_scr[:, 0:n_past] > 0.0)), 1.0, 0.0)
    sel_n = jnp.where((key_n > thr) | (eq_n & (tie_scr[:, n_past:n_past + NEW_PAD] > 0.0)), 1.0, 0.0)
    rr = lax.broadcasted_iota(I32, (N_HEADS * n_new, DEC_ROWS), 0)
    rc = lax.broadcasted_iota(I32, (N_HEADS * n_new, DEC_ROWS), 1)
    hit = rr == rc
    for hh in range(1, N_HEADS):
        hit = hit | (rr == rc + hh * n_new)
    rep = jnp.where(hit & (rc < n_new), 1.0, 0.0).astype(BF16)
    m_p = jnp.dot(rep, sel_p.astype(BF16), preferred_element_type=F32)
    m_n = jnp.dot(rep, sel_n.astype(BF16), preferred_element_type=F32)

    q = q_ref[...]
    scale = HEAD_DIM ** -0.5
    k_all = kbuf[slot].reshape(n_past, HEAD_DIM).astype(BF16)
    lg_p = jnp.where(m_p > 0.5, lax.dot_general(q, k_all, nt, preferred_element_type=F32) * scale, NEG_INF)
    lg_n = jnp.where(m_n > 0.5, lax.dot_general(q, kn_ref[...], nt, preferred_element_type=F32) * scale,
                     NEG_INF)
    mx = jnp.maximum(jnp.max(lg_p, axis=1, keepdims=True), jnp.max(lg_n, axis=1, keepdims=True))
    p_p = jnp.exp(lg_p - mx)
    p_n = jnp.exp(lg_n - mx)
    den = jnp.sum(p_p, axis=1, keepdims=True) + jnp.sum(p_n, axis=1, keepdims=True)
    v_all = vbuf[slot].reshape(n_past, HEAD_DIM).astype(BF16)
    num = (jnp.dot(p_p.astype(BF16), v_all, preferred_element_type=F32)
           + jnp.dot(p_n.astype(BF16), vn_ref[...], preferred_element_type=F32))
    o_ref[...] = num / den


def _dsa_sample_pallas(q, kb, vb, iq, ikb, iw, cache_k, cache_v, cache_ik, page_table):
    db, ds, _ = q.shape
    n_pages = page_table.shape[1]
    n_past = n_pages * PAGE_SIZE
    assert ds <= DEC_ROWS and ds <= NEW_PAD
    k_sel = min(TOPK_MAX, (n_past + ds) // 4)
    heads_first = lambda a, nh: a.reshape(db, ds, nh, -1).transpose(0, 2, 1, 3).reshape(db, nh * ds, -1)
    iq_r = heads_first(iq, IDX_HEADS)
    q_r = heads_first(q, N_HEADS)
    wsel = jnp.einsum('bjh,jk->bjhk', iw, jnp.eye(ds, dtype=F32)).reshape(db, ds, IDX_HEADS * ds)
    wsel = jnp.pad(wsel, ((0, 0), (0, DEC_ROWS - ds), (0, 0)))
    pad_new = lambda a: jnp.pad(a, ((0, 0), (0, NEW_PAD - ds), (0, 0)))
    per_seq = lambda a: pl.BlockSpec((None,) + a.shape[1:], lambda i, pt: (i, 0, 0))
    any_spec = pl.BlockSpec(memory_space=pl.ANY)
    ins = (iq_r, wsel, q_r, pad_new(ikb), pad_new(kb), pad_new(vb))
    out = pl.pallas_call(
        functools.partial(_dsa_sample_kernel, n_pages=n_pages, n_new=ds, k_sel=k_sel),
        grid_spec=pltpu.PrefetchScalarGridSpec(
            num_scalar_prefetch=1,
            grid=(db,),
            in_specs=[per_seq(a) for a in ins] + [any_spec, any_spec, any_spec],
            out_specs=pl.BlockSpec((None, N_HEADS * ds, HEAD_DIM), lambda i, pt: (i, 0, 0)),
            scratch_shapes=[
                pltpu.VMEM((2, n_pages, PAGE_SIZE, IDX_DIM), F32),
                pltpu.VMEM((2, n_pages, PAGE_SIZE, HEAD_DIM), F32),
                pltpu.VMEM((2, n_pages, PAGE_SIZE, HEAD_DIM), F32),
                pltpu.SemaphoreType.DMA((2, 3)),
                pltpu.VMEM((DEC_ROWS, n_past + NEW_PAD), F32),
            ],
        ),
        out_shape=jax.ShapeDtypeStruct((db, N_HEADS * ds, HEAD_DIM), F32),
        compiler_params=_cparams(("arbitrary",)),
        name="dsa_sample",
    )(page_table, *ins, cache_ik, cache_k, cache_v)
    return out.reshape(db, N_HEADS, ds, HEAD_DIM).transpose(0, 2, 1, 3).reshape(db, ds, N_HEADS * HEAD_DIM)


def kernel(x_prompt, x_sample, cache_k, cache_v, cache_ik, state_pool, page_table, norm1_g, w_in,
           q_norm_g, k_norm_g, w_pool, pool_scale, w_o, norm2_g, w_pq, sub_keys, expert_u, expert_v):
    b, s, d = x_prompt.shape
    db, ds, _ = x_sample.shape
    past = page_table.shape[1] * PAGE_SIZE
    w_p = _pack_w_in(w_in[0])
    row = lambda a: a[0][None, :]
    shared = (row(norm1_g), w_p, row(q_norm_g), row(k_norm_g), w_pool[0].astype(BF16), row(pool_scale))

    (ya, q, k1, v1, kb, vbt, iq, ik1, ikb, iwt, pool_p) = _inproj_prompt(
        x_prompt, _rope_tables(jnp.arange(s)), *shared)
    yb = _dsa_prompt_pallas(q, kb, vbt, iq, ikb, iwt)

    x_tm = x_sample.transpose(1, 0, 2).reshape(ds * db, d)
    tab_s = _rope_tables(past + jnp.repeat(jnp.arange(ds), db))
    (ya_s, q_s, k2, v2, kb_s, vb_s, iq_s, ik2, ikb_s, iw_s, pool_s) = _inproj_sample(
        x_tm, tab_s, *shared, state_pool[0].transpose(1, 0, 2), db, ds, past)
    bm = lambda a: a.reshape(ds, db, -1).transpose(1, 0, 2)
    k2, v2, ik2 = bm(k2), bm(v2), bm(ik2)
    yb_s = _dsa_sample_pallas(bm(q_s), bm(kb_s), bm(vb_s), bm(iq_s), bm(ikb_s), bm(iw_s),
                              cache_k[0], cache_v[0], cache_ik[0], page_table).astype(BF16)

    n_p, n_s = b * s, db * ds
    assert n_s <= OUT_TM and OUT_TM == PEER_TM
    flat = lambda a: a.reshape(n_p, -1)
    blk = lambda a: jnp.pad(a.reshape(n_s, -1), ((0, OUT_TM - n_s), (0, 0)))
    x2, h2t = _outproj(flat(x_prompt), flat(ya), flat(yb), blk(x_sample), blk(bm(ya_s)), blk(yb_s),
                       w_o[0], row(norm2_g))
    y_p, y_s = _peer(x2, h2t, w_pq[0], sub_keys[0], expert_u[0], expert_v[0])
    st = lambda a: a[None]
    return (y_p.reshape(b, s, d), y_s[:n_s].reshape(db, ds, d),
            st(k1), st(v1), st(ik1), st(pool_p[:, 1:]),
            st(k2), st(v2), st(ik2), st(pool_s.transpose(1, 0, 2)))
```
